```python
import jax
import jax.numpy as jnp
from jax import lax
import numpy as np

D_MODEL = 2048
BATCH = 2
SEQ = 16384
DEPTH = 4

CHUNK = 64
Q_BLOCK = 128
EPS = 1e-6

M_HEADS = 4
M_DQK = 128
M_DV = 256
F_HEADS = 8
F_DH = 128
G_HEADS = 4
G_DK = 256
G_DV = 512
G_RANK = 16
G_TAU = 16.0
D_FF = 5632
CONV_W = 3

N_EVEN = (DEPTH + 1) // 2
N_ODD = DEPTH // 2

EVEN_SIZES = (M_HEADS * M_DQK, M_HEADS * M_DQK, M_HEADS * M_DV, M_HEADS * M_DV, 2 * M_HEADS,
              F_HEADS * F_DH, F_HEADS * F_DH, F_HEADS * F_DH, F_HEADS)
EVEN_IN = sum(EVEN_SIZES)
EVEN_OUT = M_HEADS * M_DV + F_HEADS * F_DH
ODD_SIZES = (G_HEADS * G_DK, G_HEADS * G_DK, G_HEADS * G_DV, G_HEADS * G_DV, G_RANK)
ODD_IN = sum(ODD_SIZES)
ODD_OUT = G_HEADS * G_DV

kernel_name = "hybrid_mlstm_fox_gla_convffn_block"


def _split(u, sizes):
    return jnp.split(u, np.cumsum(sizes)[:-1].tolist(), axis=-1)


def rms_norm(x, w):
    xf = x.astype(jnp.float32)
    y = xf * lax.rsqrt(jnp.mean(xf * xf, axis=-1, keepdims=True) + EPS)
    return (y * w.astype(jnp.float32)).astype(x.dtype)


def _to_chunks(a):
    b, t, h = a.shape[:3]
    a = a.reshape((b, t // CHUNK, CHUNK, h) + a.shape[3:])
    return jnp.moveaxis(a, (1, 3), (0, 2))


def _from_chunks(a):
    a = jnp.moveaxis(a, (0, 2), (1, 3))
    b, nc, l, h = a.shape[:4]
    return a.reshape((b, nc * l, h) + a.shape[4:])


def mlstm(q, k, v, i_pre, f_pre):
    b, t, h, dk = q.shape
    dv = v.shape[-1]
    f32 = jnp.float32
    qc = _to_chunks(q.astype(f32) * dk ** -0.5)
    kc = _to_chunks(k.astype(f32))
    vc = _to_chunks(v.astype(f32))
    lfc = _to_chunks(jax.nn.log_sigmoid(f_pre.astype(f32)))
    igc = _to_chunks(i_pre.astype(f32))
    causal = jnp.tril(jnp.ones((CHUNK, CHUNK), dtype=bool))

    def step(carry, inp):
        c_st, n_st, m_st = carry
        qb, kb, vb, lf, ig = inp
        cum = jnp.cumsum(lf, axis=-1)
        dmat = cum[..., :, None] - cum[..., None, :] + ig[..., None, :]
        dmat = jnp.where(causal, dmat, -jnp.inf)
        inter = cum + m_st[..., None]
        m_t = jnp.maximum(inter, jnp.max(dmat, axis=-1))
        s = jnp.einsum("bhtd,bhsd->bhts", qb, kb) * jnp.exp(dmat - m_t[..., None])
        g = jnp.exp(inter - m_t)
        num = (jnp.einsum("bhts,bhsv->bhtv", s, vb)
               + g[..., None] * jnp.einsum("bhtd,bhdv->bhtv", qb, c_st))
        den = jnp.sum(s, axis=-1) + g * jnp.einsum("bhtd,bhd->bht", qb, n_st)
        h_out = num / jnp.maximum(jnp.abs(den), jnp.exp(-m_t))[..., None]
        last = cum[..., -1]
        dec = last[..., None] - cum + ig
        m_new = jnp.maximum(last + m_st, jnp.max(dec, axis=-1))
        ws = jnp.exp(dec - m_new[..., None])
        gs = jnp.exp(last + m_st - m_new)
        c_new = gs[..., None, None] * c_st + jnp.einsum("bhs,bhsd,bhsv->bhdv", ws, kb, vb)
        n_new = gs[..., None] * n_st + jnp.einsum("bhs,bhsd->bhd", ws, kb)
        return (c_new, n_new, m_new), h_out

    init = (jnp.zeros((b, h, dk, dv), f32), jnp.zeros((b, h, dk), f32), jnp.zeros((b, h), f32))
    _, hs = lax.scan(step, init, (qc, kc, vc, lfc, igc))
    return _from_chunks(hs)


def forgetting_attention(q, k, v, f_pre):
    b, t, h, d = q.shape
    cum = jnp.cumsum(jax.nn.log_sigmoid(f_pre.astype(jnp.float32)), axis=1).transpose(0, 2, 1)
    nb = t // Q_BLOCK
    q_blocks = q.reshape(b, nb, Q_BLOCK, h, d).transpose(1, 0, 2, 3, 4)
    cum_blocks = cum.reshape(b, h, nb, Q_BLOCK).transpose(2, 0, 1, 3)
    starts = jnp.arange(nb, dtype=jnp.int32) * Q_BLOCK
    key_pos = jnp.arange(t, dtype=jnp.int32)
    scale = d ** -0.5

    def one_block(args):
        qi, cum_i, start = args
        logits = jnp.einsum("bqhd,bkhd->bhqk", qi, k, preferred_element_type=jnp.float32) * scale
        logits = logits + (cum_i[..., :, None] - cum[..., None, :])
        q_pos = start + jnp.arange(Q_BLOCK, dtype=jnp.int32)
        logits = jnp.where(q_pos[:, None] >= key_pos[None, :], logits, -jnp.inf)
        p = jax.nn.softmax(logits, axis=-1)
        return jnp.einsum("bhqk,bkhd->bqhd", p.astype(v.dtype), v)

    out = lax.map(one_block, (q_blocks, cum_blocks, starts))
    return out.transpose(1, 0, 2, 3, 4).reshape(b, t, h, d)


def gla(q, k, v, log_a):
    b, t, h, dk = q.shape
    dv = v.shape[-1]
    f32 = jnp.float32
    qc = _to_chunks(q.astype(f32) * dk ** -0.5)
    kc = _to_chunks(k.astype(f32))
    vc = _to_chunks(v.astype(f32))
    lac = _to_chunks(log_a.astype(f32))
    causal = jnp.tril(jnp.ones((CHUNK, CHUNK), dtype=bool))[:, :, None]

    def step(s_st, inp):
        qb, kb, vb, la = inp
        cum = jnp.cumsum(la, axis=2)
        diff = cum[:, :, :, None, :] - cum[:, :, None, :, :]
        decay = jnp.exp(jnp.where(causal, diff, -jnp.inf))
        attn = jnp.einsum("bhtd,bhsd,bhtsd->bhts", qb, kb, decay)
        o = (jnp.einsum("bhts,bhsv->bhtv", attn, vb)
             + jnp.einsum("bhtd,bhdv->bhtv", qb * jnp.exp(cum), s_st))
        last = cum[:, :, -1:, :]
        s_new = (jnp.exp(last[:, :, 0, :])[..., None] * s_st
                 + jnp.einsum("bhsd,bhsv->bhdv", kb * jnp.exp(last - cum), vb))
        return s_new, o

    _, os_ = lax.scan(step, jnp.zeros((b, h, dk, dv), f32), (qc, kc, vc, lac))
    return _from_chunks(os_)


def even_mixer(h, w_in, m_gate_b, m_norm_w, fox_f_b, w_out):
    b, t, _ = h.shape
    u = h @ w_in
    mq, mk, mv, mo, mg, fq, fk, fv, ff = _split(u, EVEN_SIZES)
    mg = mg + m_gate_b
    hm = mlstm(mq.reshape(b, t, M_HEADS, M_DQK), mk.reshape(b, t, M_HEADS, M_DQK),
               mv.reshape(b, t, M_HEADS, M_DV), mg[..., :M_HEADS], mg[..., M_HEADS:])
    hm = rms_norm(hm.astype(h.dtype), m_norm_w.reshape(M_HEADS, M_DV))
    hm = hm * jax.nn.sigmoid(mo).reshape(b, t, M_HEADS, M_DV)
    hf = forgetting_attention(fq.reshape(b, t, F_HEADS, F_DH), fk.reshape(b, t, F_HEADS, F_DH),
                              fv.reshape(b, t, F_HEADS, F_DH), ff + fox_f_b)
    y = jnp.concatenate([hm.reshape(b, t, -1), hf.reshape(b, t, -1)], axis=-1)
    return y @ w_out


def odd_mixer(h, w_in, a_w2, a_b, g_norm_w, w_out):
    b, t, _ = h.shape
    u = h @ w_in
    gq, gk, gv, gr, ga = _split(u, ODD_SIZES)
    log_a = jax.nn.log_sigmoid((ga @ a_w2 + a_b).astype(jnp.float32)) / G_TAU
    o = gla(gq.reshape(b, t, G_HEADS, G_DK), gk.reshape(b, t, G_HEADS, G_DK),
            gv.reshape(b, t, G_HEADS, G_DV), log_a.reshape(b, t, G_HEADS, G_DK))
    o = rms_norm(o.astype(h.dtype), g_norm_w.reshape(G_HEADS, G_DV))
    o = o * jax.nn.silu(gr).reshape(b, t, G_HEADS, G_DV)
    return o.reshape(b, t, -1) @ w_out


def conv_ffn(h, w_in, conv_w, conv_b, w_out):
    u = h @ w_in
    u = lax.conv_general_dilated(u, conv_w, window_strides=(1,), padding=((CONV_W - 1, 0),),
                                 dimension_numbers=("NWC", "WIO", "NWC"),
                                 feature_group_count=2 * D_FF) + conv_b
    a, g = jnp.split(u, 2, axis=-1)
    return (a * jax.nn.gelu(g, approximate=True)) @ w_out


def setup_inputs(seed: int = 0) -> dict:
    key = jax.random.key(seed)
    ks = jax.random.split(key, 24)
    nrm = jax.random.normal
    d = D_MODEL
    x = nrm(ks[0], (BATCH, SEQ, d), jnp.float32)
    c = nrm(ks[1], (BATCH, d), jnp.float32)
    ada_w = nrm(ks[2], (DEPTH, d, 6 * d), jnp.float32) * (0.5 * d ** -0.5)
    ada_b = 0.02 * nrm(ks[3], (DEPTH, 6 * d), jnp.float32)
    norm_w = 1.0 + 0.05 * nrm(ks[4], (DEPTH, 4, d), jnp.float32)
    even_in_w = nrm(ks[5], (N_EVEN, d, EVEN_IN), jnp.float32) * d ** -0.5
    i_b = 0.1 * nrm(ks[6], (N_EVEN, M_HEADS), jnp.float32)
    f_b = jnp.linspace(3.0, 6.0, M_HEADS)[None, :] + 0.1 * nrm(ks[7], (N_EVEN, M_HEADS), jnp.float32)
    m_gate_b = jnp.concatenate([i_b, f_b], axis=-1)
    m_norm_w = 1.0 + 0.05 * nrm(ks[8], (N_EVEN, M_HEADS * M_DV), jnp.float32)
    fox_f_b = jnp.linspace(1.0, 4.0, F_HEADS)[None, :] + 0.1 * nrm(ks[9], (N_EVEN, F_HEADS), jnp.float32)
    even_out_w = nrm(ks[10], (N_EVEN, EVEN_OUT, d), jnp.float32) * EVEN_OUT ** -0.5
    odd_in_w = nrm(ks[11], (N_ODD, d, ODD_IN), jnp.float32) * d ** -0.5
    gla_a_w2 = nrm(ks[12], (N_ODD, G_RANK, G_HEADS * G_DK), jnp.float32) * G_RANK ** -0.5
    gla_a_b = 0.1 * nrm(ks[13], (N_ODD, G_HEADS * G_DK), jnp.float32)
    gla_norm_w = 1.0 + 0.05 * nrm(ks[14], (N_ODD, G_HEADS * G_DV), jnp.float32)
    odd_out_w = nrm(ks[15], (N_ODD, ODD_OUT, d), jnp.float32) * ODD_OUT ** -0.5
    ffn_in_w = nrm(ks[16], (DEPTH, d, 2 * D_FF), jnp.float32) * d ** -0.5
    ffn_conv_w = nrm(ks[17], (DEPTH, CONV_W, 1, 2 * D_FF), jnp.float32) * CONV_W ** -0.5
    ffn_conv_b = 0.02 * nrm(ks[18], (DEPTH, 2 * D_FF), jnp.float32)
    ffn_out_w = nrm(ks[19], (DEPTH, D_FF, d), jnp.float32) * D_FF ** -0.5
    return {"x": x, "c": c, "ada_w": ada_w, "ada_b": ada_b, "norm_w": norm_w,
            "even_in_w": even_in_w, "m_gate_b": m_gate_b, "m_norm_w": m_norm_w,
            "fox_f_b": fox_f_b, "even_out_w": even_out_w, "odd_in_w": odd_in_w,
            "gla_a_w2": gla_a_w2, "gla_a_b": gla_a_b, "gla_norm_w": gla_norm_w,
            "odd_out_w": odd_out_w, "ffn_in_w": ffn_in_w, "ffn_conv_w": ffn_conv_w,
            "ffn_conv_b": ffn_conv_b, "ffn_out_w": ffn_out_w}


def reference(x, c, ada_w, ada_b, norm_w, even_in_w, m_gate_b, m_norm_w, fox_f_b, even_out_w,
              odd_in_w, gla_a_w2, gla_a_b, gla_norm_w, odd_out_w, ffn_in_w, ffn_conv_w,
              ffn_conv_b, ffn_out_w):
    cond = jax.nn.silu(c)
    for layer in range(DEPTH):
        mod = (cond @ ada_w[layer] + ada_b[layer])[:, None, :]
        sh_m, sc_m, g_m, sh_f, sc_f, g_f = jnp.split(mod, 6, axis=-1)
        h = rms_norm(x, norm_w[layer, 0]) * (1.0 + sc_m) + sh_m
        j = layer // 2
        if layer % 2 == 0:
            y = even_mixer(h, even_in_w[j], m_gate_b[j], m_norm_w[j], fox_f_b[j], even_out_w[j])
        else:
            y = odd_mixer(h, odd_in_w[j], gla_a_w2[j], gla_a_b[j], gla_norm_w[j], odd_out_w[j])
        x = x + g_m * rms_norm(y, norm_w[layer, 1])
        h = rms_norm(x, norm_w[layer, 2]) * (1.0 + sc_f) + sh_f
        y = conv_ffn(h, ffn_in_w[layer], ffn_conv_w[layer], ffn_conv_b[layer], ffn_out_w[layer])
        x = x + g_f * rms_norm(y, norm_w[layer, 3])
    return x
```

```python
import functools

import jax
import jax.numpy as jnp
from jax import lax
from jax.experimental import pallas as pl
from jax.experimental.pallas import tpu as pltpu

F32 = jnp.float32
BF16 = jnp.bfloat16

LANE = 128
SUBLANE = 8
VMEM_BYTES_V7X = 64 * 1024 * 1024

D_MODEL = 2048
DEPTH = 4
EPS = 1e-6
M_HEADS, M_DQK, M_DV = 4, 128, 256
F_HEADS, F_DH = 8, 128
G_HEADS, G_DK, G_DV = 4, 256, 512
G_RANK = 16
G_TAU = 16.0
D_FF = 5632
CONV_W = 3

MIX_W = 6144
GATE_W = LANE
N_GATE_ROWS = 16
M_CHUNK = 256
G_BLOCK = 256
G_CHUNK = 64
G_SUB = 16
FOX_T = 512
CONV_HALO = SUBLANE


def _cparams(sem, vmem_mb):
    return pltpu.CompilerParams(dimension_semantics=sem, vmem_limit_bytes=vmem_mb * 1024 * 1024)


def _dot(a, b):
    return jnp.dot(a, b, preferred_element_type=F32)


def _dot_nt(a, b):
    return lax.dot_general(a, b, (((1,), (1,)), ((), ())), preferred_element_type=F32)


def _dot_tn(a, b):
    return lax.dot_general(a, b, (((0,), (0,)), ((), ())), preferred_element_type=F32)


def _split3(x):
    hi = x.astype(BF16)
    r = x - hi.astype(F32)
    mid = r.astype(BF16)
    lo = (r - mid.astype(F32)).astype(BF16)
    return hi, mid, lo


def _cumsum_rows(tri_lower, x):
    hi, mid, lo = _split3(x)
    return _dot(tri_lower, hi) + _dot(tri_lower, mid) + _dot(tri_lower, lo)


def _cumsum_lanes(x, tri_upper):
    hi, mid, lo = _split3(x)
    return _dot(hi, tri_upper) + _dot(mid, tri_upper) + _dot(lo, tri_upper)


def _tri(n, lower):
    r = lax.broadcasted_iota(jnp.int32, (n, n), 0)
    c = lax.broadcasted_iota(jnp.int32, (n, n), 1)
    return jnp.where((c <= r) if lower else (r <= c), 1.0, 0.0).astype(BF16)


def _log_sigmoid(z):
    return jnp.minimum(z, 0.0) - jnp.log1p(jnp.exp(-jnp.abs(z)))


def _rms(x):
    return x * lax.rsqrt(jnp.mean(x * x, axis=-1, keepdims=True) + EPS)


def _ada_kernel(ct_ref, w_ref, b_ref, o_ref):
    d, bsz = ct_ref.shape
    tn = w_ref.shape[2]
    rows_per_step = 512

    def body(r, accs):
        rows = pl.ds(pl.multiple_of(r * rows_per_step, rows_per_step), rows_per_step)
        ct = ct_ref[rows, :]
        cond = ct * jax.nn.sigmoid(ct)
        w = w_ref[0, rows, :]
        return tuple(accs[b] + jnp.sum(cond[:, b:b + 1] * w, axis=0, keepdims=True) for b in range(bsz))

    accs = lax.fori_loop(0, d // rows_per_step, body, tuple(jnp.zeros((1, tn), F32) for _ in range(bsz)))
    for b in range(bsz):
        o_ref[0, b:b + 1, :] = accs[b] + b_ref[0]


def _ada_mod(c, ada_w, ada_b):
    bsz, d = c.shape
    depth, _, n = ada_w.shape
    tn = 1024
    return pl.pallas_call(
        _ada_kernel,
        grid=(depth, n // tn),
        in_specs=[pl.BlockSpec((d, bsz), lambda l, j: (0, 0)),
                  pl.BlockSpec((1, d, tn), lambda l, j: (l, 0, j)),
                  pl.BlockSpec((1, 1, tn), lambda l, j: (l, 0, j))],
        out_specs=pl.BlockSpec((1, bsz, tn), lambda l, j: (l, 0, j)),
        out_shape=jax.ShapeDtypeStruct((depth, bsz, n), F32),
        compiler_params=_cparams(("arbitrary", "arbitrary"), 40),
        name="ada_mod",
    )(c.T, ada_w, ada_b.reshape(depth, 1, n))


def _prenorm_kernel(x_ref, w_ref, sc_ref, sh_ref, h_ref):
    y = _rms(x_ref[...]) * w_ref[...]
    h_ref[...] = (y * (1.0 + sc_ref[0]) + sh_ref[0]).astype(BF16)


def _prenorm(x2, w, sc, sh, seq, tm=512):
    n, d = x2.shape
    per_b = seq // tm
    vec = pl.BlockSpec((1, 1, d), lambda i: (i // per_b, 0, 0))
    return pl.pallas_call(
        _prenorm_kernel,
        grid=(n // tm,),
        in_specs=[pl.BlockSpec((tm, d), lambda i: (i, 0)),
                  pl.BlockSpec((1, d), lambda i: (0, 0)), vec, vec],
        out_specs=pl.BlockSpec((tm, d), lambda i: (i, 0)),
        out_shape=jax.ShapeDtypeStruct((n, d), BF16),
        compiler_params=_cparams(("arbitrary",), 32),
        name="prenorm",
    )(x2, w.reshape(1, d), sc, sh)


def _inproj_kernel(h_ref, w_ref, wg_ref, wgt_ref, o_ref, gcol_ref, *grow_ref):
    h = h_ref[...]
    o_ref[...] = _dot(h, w_ref[...]).astype(BF16)

    @pl.when(pl.program_id(1) == 0)
    def _():
        gcol_ref[...] = _dot(h, wg_ref[...])
        if grow_ref:
            grow_ref[0][...] = _dot_nt(wgt_ref[...], h)


def _inproj(h, w, wg, wgt, want_rows, tm=1024, tn=512):
    n, d = h.shape
    ncol = w.shape[1]
    out_shape = [jax.ShapeDtypeStruct((n, ncol), BF16), jax.ShapeDtypeStruct((n, GATE_W), F32)]
    out_specs = [pl.BlockSpec((tm, tn), lambda i, j: (i, j)),
                 pl.BlockSpec((tm, GATE_W), lambda i, j: (i, 0))]
    if want_rows:
        out_shape.append(jax.ShapeDtypeStruct((N_GATE_ROWS, n), F32))
        out_specs.append(pl.BlockSpec((N_GATE_ROWS, tm), lambda i, j: (0, i)))
    return pl.pallas_call(
        _inproj_kernel,
        grid=(n // tm, ncol // tn),
        in_specs=[pl.BlockSpec((tm, d), lambda i, j: (i, 0)),
                  pl.BlockSpec((d, tn), lambda i, j: (0, j)),
                  pl.BlockSpec((d, GATE_W), lambda i, j: (0, 0)),
                  pl.BlockSpec((N_GATE_ROWS, d), lambda i, j: (0, 0))],
        out_specs=out_specs,
        out_shape=out_shape,
        compiler_params=_cparams(("arbitrary", "arbitrary"), 40),
        name="inproj",
    )(h, w, wg, wgt)


def _mlstm_kernel(q_ref, k_ref, v_ref, og_ref, gcol_ref, grow_ref, bcol_ref, brow_ref, nw_ref,
                  y_ref, fcol_ref, frow_ref, c_sc, n_sc, m_sc, carry_col, carry_row):
    L = M_CHUNK

    @pl.when(pl.program_id(1) == 0)
    def _():
        c_sc[...] = jnp.zeros_like(c_sc)
        n_sc[...] = jnp.zeros_like(n_sc)
        m_sc[...] = jnp.zeros_like(m_sc)
        carry_col[...] = jnp.zeros_like(carry_col)
        carry_row[...] = jnp.zeros_like(carry_row)

    gcol = gcol_ref[0] + bcol_ref[...]
    grow = grow_ref[...] + brow_ref[:, 0:1]
    tri_l = _tri(L, True)
    tri_u = _tri(L, False)
    cum_col = _cumsum_rows(tri_l, _log_sigmoid(gcol))
    cum_row = _cumsum_lanes(_log_sigmoid(grow), tri_u)

    fcol = cum_col + carry_col[0:1, :]
    frow = cum_row + carry_row[:, 0:1]
    fcol_ref[0] = fcol
    frow_ref[0] = frow[M_HEADS * 2:, :]
    carry_col[...] = jnp.broadcast_to(fcol[L - 1:L, :], carry_col.shape)
    carry_row[...] = jnp.broadcast_to(frow[:, L - 1:L], carry_row.shape)

    causal = lax.broadcasted_iota(jnp.int32, (L, L), 0) >= lax.broadcasted_iota(jnp.int32, (L, L), 1)
    for hd in range(M_HEADS):
        q = q_ref[0, :, hd * M_DQK:(hd + 1) * M_DQK]
        k = k_ref[0, :, hd * M_DQK:(hd + 1) * M_DQK]
        v = v_ref[0, :, hd * M_DV:(hd + 1) * M_DV]
        ig_c = gcol[:, hd:hd + 1]
        ig_r = grow[hd:hd + 1, :]
        cum_c = cum_col[:, M_HEADS + hd:M_HEADS + hd + 1]
        cum_r = cum_row[M_HEADS + hd:M_HEADS + hd + 1, :]
        m_prev = m_sc[hd, 0:1, 0:1]
        c_st = c_sc[hd]
        n_st = n_sc[hd, 0:1, :]

        dmat = jnp.where(causal, cum_c - cum_r + ig_r, -jnp.inf)
        inter = cum_c + m_prev
        m_t = jnp.maximum(inter, jnp.max(dmat, axis=-1, keepdims=True))
        s = _dot_nt(q, k) * jnp.exp(dmat - m_t)
        g = jnp.exp(inter - m_t)
        qf = q.astype(F32)
        num = _dot(s.astype(BF16), v) + g * _dot(q, c_st.astype(BF16))
        den = jnp.sum(s, axis=-1, keepdims=True) + g * jnp.sum(qf * n_st, axis=-1, keepdims=True)
        h_out = num / jnp.maximum(jnp.abs(den), jnp.exp(-m_t))

        last = cum_c[L - 1:L, :]
        dec = last - cum_c + ig_c
        m_new = jnp.maximum(last + m_prev, jnp.max(dec, axis=0, keepdims=True))
        ws = jnp.exp(dec - m_new)
        gs = jnp.exp(last + m_prev - m_new)
        kw = k.astype(F32) * ws
        c_sc[hd] = gs * c_st + _dot_tn(kw.astype(BF16), v)
        n_sc[hd] = jnp.broadcast_to(gs * n_st + jnp.sum(kw, axis=0, keepdims=True), n_sc.shape[1:])
        m_sc[hd] = jnp.broadcast_to(m_new, m_sc.shape[1:])

        og = og_ref[0, :, hd * M_DV:(hd + 1) * M_DV].astype(F32)
        hn = _rms(h_out) * nw_ref[:, hd * M_DV:(hd + 1) * M_DV]
        y_ref[0, :, hd * M_DV:(hd + 1) * M_DV] = (hn * jax.nn.sigmoid(og)).astype(BF16)


def _mlstm(u, gcol, grow, bias_col, bias_row, norm_w):
    bsz, seq, _ = u.shape
    L = M_CHUNK
    nc = seq // L
    qk_w = M_HEADS * M_DQK
    v_w = M_HEADS * M_DV
    return pl.pallas_call(
        _mlstm_kernel,
        grid=(bsz, nc),
        in_specs=[pl.BlockSpec((1, L, qk_w), lambda b, c: (b, c, 0)),
                  pl.BlockSpec((1, L, qk_w), lambda b, c: (b, c, 1)),
                  pl.BlockSpec((1, L, v_w), lambda b, c: (b, c, 1)),
                  pl.BlockSpec((1, L, v_w), lambda b, c: (b, c, 2)),
                  pl.BlockSpec((1, L, GATE_W), lambda b, c: (b, c, 0)),
                  pl.BlockSpec((N_GATE_ROWS, L), lambda b, c: (0, b * nc + c)),
                  pl.BlockSpec((1, GATE_W), lambda b, c: (0, 0)),
                  pl.BlockSpec((N_GATE_ROWS, LANE), lambda b, c: (0, 0)),
                  pl.BlockSpec((1, v_w), lambda b, c: (0, 0))],
        out_specs=[pl.BlockSpec((1, L, v_w), lambda b, c: (b, c, 0)),
                   pl.BlockSpec((1, L, GATE_W), lambda b, c: (b, c, 0)),
                   pl.BlockSpec((1, F_HEADS, L), lambda b, c: (b, 0, c))],
        out_shape=[jax.ShapeDtypeStruct((bsz, seq, v_w), BF16),
                   jax.ShapeDtypeStruct((bsz, seq, GATE_W), F32),
                   jax.ShapeDtypeStruct((bsz, F_HEADS, seq), F32)],
        scratch_shapes=[pltpu.VMEM((M_HEADS, M_DQK, M_DV), F32),
                        pltpu.VMEM((M_HEADS, SUBLANE, M_DQK), F32),
                        pltpu.VMEM((M_HEADS, SUBLANE, LANE), F32),
                        pltpu.VMEM((SUBLANE, GATE_W), F32),
                        pltpu.VMEM((N_GATE_ROWS, LANE), F32)],
        compiler_params=_cparams(("arbitrary", "arbitrary"), 40),
        name="mlstm",
    )(u, u, u, u, gcol, grow, bias_col, bias_row, norm_w)


def _fox_kernel(q_ref, k_ref, v_ref, fcol_ref, frow_ref, o_ref, m_sc, l_sc, acc_sc):
    tq = FOX_T
    hd = pl.program_id(1)
    qi = pl.program_id(2)
    q = q_ref[0]
    lane = lax.broadcasted_iota(jnp.int32, (tq, GATE_W), 1)
    cq = jnp.sum(jnp.where(lane == 2 * M_HEADS + hd, fcol_ref[0], 0.0), axis=-1, keepdims=True)

    m_sc[...] = jnp.full_like(m_sc, -jnp.inf)
    l_sc[...] = jnp.zeros_like(l_sc)
    acc_sc[...] = jnp.zeros_like(acc_sc)

    def block(j, masked):
        off = pl.multiple_of(j * tq, tq)
        kj = k_ref[0, pl.ds(off, tq), :]
        vj = v_ref[0, pl.ds(off, tq), :]
        ck = frow_ref[0, pl.ds(hd, 1), pl.ds(off, tq)]
        s = _dot_nt(q, kj) + (cq - ck)
        if masked:
            r = lax.broadcasted_iota(jnp.int32, (tq, tq), 0)
            c = lax.broadcasted_iota(jnp.int32, (tq, tq), 1)
            s = jnp.where(r >= c, s, -jnp.inf)
        m_old = m_sc[...]
        m_new = jnp.maximum(m_old, jnp.max(s, axis=-1, keepdims=True))
        alpha = jnp.exp(m_old - m_new)
        p = jnp.exp(s - m_new)
        l_sc[...] = alpha * l_sc[...] + jnp.sum(p, axis=-1, keepdims=True)
        acc_sc[...] = alpha * acc_sc[...] + _dot(p.astype(BF16), vj)
        m_sc[...] = m_new

    def body(j, carry):
        block(j, False)
        return carry

    lax.fori_loop(0, qi, body, 0)
    block(qi, True)
    o_ref[0] = (acc_sc[...] / l_sc[...]).astype(BF16)


def _fox(u, fcol, frow):
    bsz, seq, _ = u.shape
    tq = FOX_T
    q_blk = (M_HEADS * (2 * M_DQK + 2 * M_DV)) // F_DH
    k_blk = q_blk + F_HEADS
    v_blk = k_blk + F_HEADS
    return pl.pallas_call(
        _fox_kernel,
        grid=(bsz, F_HEADS, seq // tq),
        in_specs=[pl.BlockSpec((1, tq, F_DH), lambda b, h, i: (b, i, q_blk + h)),
                  pl.BlockSpec((1, seq, F_DH), lambda b, h, i: (b, 0, k_blk + h)),
                  pl.BlockSpec((1, seq, F_DH), lambda b, h, i: (b, 0, v_blk + h)),
                  pl.BlockSpec((1, tq, GATE_W), lambda b, h, i: (b, i, 0)),
                  pl.BlockSpec((1, F_HEADS, seq), lambda b, h, i: (b, 0, 0))],
        out_specs=pl.BlockSpec((1, tq, F_DH), lambda b, h, i: (b, i, h)),
        out_shape=jax.ShapeDtypeStruct((bsz, seq, F_HEADS * F_DH), BF16),
        scratch_shapes=[pltpu.VMEM((tq, 1), F32), pltpu.VMEM((tq, 1), F32),
                        pltpu.VMEM((tq, F_DH), F32)],
        compiler_params=_cparams(("arbitrary", "arbitrary", "arbitrary"), 48),
        name="fox",
    )(u, u, u, fcol, frow)


def _gla_kernel(q_ref, k_ref, v_ref, og_ref, ga_ref, aw_ref, ab_ref, nw_ref, o_ref, st_sc):
    L, S = G_CHUNK, G_SUB
    nsub = L // S

    @pl.when(pl.program_id(2) == 0)
    def _():
        st_sc[...] = jnp.zeros_like(st_sc)

    tri_l = _tri(L, True)
    row = lax.broadcasted_iota(jnp.int32, (L, 1), 0)
    lane_l = lax.broadcasted_iota(jnp.int32, (S, L), 1)
    row_s = lax.broadcasted_iota(jnp.int32, (S, L), 0)

    def chunk(cc, carry):
        r0 = pl.multiple_of(cc * L, L)
        rows = pl.ds(r0, L)
        q = q_ref[0, rows, :].astype(F32) * (G_DK ** -0.5)
        k = k_ref[0, rows, :].astype(F32)
        v = v_ref[0, rows, :]
        z = _dot(ga_ref[0, rows, :].astype(BF16), aw_ref[...]) + ab_ref[...]
        la = _log_sigmoid(z) * (1.0 / G_TAU)
        c = _cumsum_rows(tri_l, la)
        c_last = c[L - 1:L, :]
        st = st_sc[...]

        o = _dot_nt((q * jnp.exp(c)).astype(BF16), st.astype(BF16))

        ref_c = c
        for i in range(1, nsub):
            ref_c = jnp.where(row >= i * S, c[i * S - 1:i * S, :], ref_c)
        qt = (q * jnp.exp(c - ref_c)).astype(BF16)
        a_rows = [jnp.zeros((S, L), F32)]
        for i in range(1, nsub):
            e = jnp.where(row < i * S, c[i * S - 1:i * S, :] - c, -jnp.inf)
            kt = (k * jnp.exp(e)).astype(BF16)
            a_rows.append(_dot_nt(qt[i * S:(i + 1) * S, :], kt))

        for i in range(nsub):
            qi = q[i * S:(i + 1) * S, :]
            ci = c[i * S:(i + 1) * S, :]
            a_d = a_rows[i]
            for sl in range(S):
                s_abs = i * S + sl
                e = jnp.minimum(ci - c[s_abs:s_abs + 1, :], 0.0)
                col = jnp.sum(qi * (k[s_abs:s_abs + 1, :] * jnp.exp(e)), axis=-1, keepdims=True)
                a_d = jnp.where((lane_l == s_abs) & (row_s >= sl), col, a_d)
            a_rows[i] = a_d
        a = jnp.concatenate(a_rows, axis=0)
        o = o + _dot(a.astype(BF16), v)

        kd = (k * jnp.exp(c_last - c)).astype(BF16)
        st_sc[...] = st * jnp.exp(c_last) + _dot_tn(v, kd)

        og = og_ref[0, rows, :].astype(F32)
        on = _rms(o) * nw_ref[...]
        o_ref[0, rows, :] = (on * (og * jax.nn.sigmoid(og))).astype(BF16)
        return carry

    lax.fori_loop(0, G_BLOCK // L, chunk, 0)


def _gla(u, gcol, aw2p, ab, norm_w):
    bsz, seq, _ = u.shape
    blk = G_BLOCK
    kq = (G_HEADS * G_DK) // G_DK
    kv = (2 * G_HEADS * G_DK) // G_DV
    return pl.pallas_call(
        _gla_kernel,
        grid=(bsz, G_HEADS, seq // blk),
        in_specs=[pl.BlockSpec((1, blk, G_DK), lambda b, h, c: (b, c, h)),
                  pl.BlockSpec((1, blk, G_DK), lambda b, h, c: (b, c, kq + h)),
                  pl.BlockSpec((1, blk, G_DV), lambda b, h, c: (b, c, kv + h)),
                  pl.BlockSpec((1, blk, G_DV), lambda b, h, c: (b, c, kv + G_HEADS + h)),
                  pl.BlockSpec((1, blk, GATE_W), lambda b, h, c: (b, c, 0)),
                  pl.BlockSpec((GATE_W, G_DK), lambda b, h, c: (0, h)),
                  pl.BlockSpec((1, G_DK), lambda b, h, c: (0, h)),
                  pl.BlockSpec((1, G_DV), lambda b, h, c: (0, h))],
        out_specs=pl.BlockSpec((1, blk, G_DV), lambda b, h, c: (b, c, h)),
        out_shape=jax.ShapeDtypeStruct((bsz, seq, G_HEADS * G_DV), BF16),
        scratch_shapes=[pltpu.VMEM((G_DV, G_DK), F32)],
        compiler_params=_cparams(("arbitrary", "arbitrary", "arbitrary"), 40),
        name="gla",
    )(u, u, u, u, gcol, aw2p, ab, norm_w)


def _outproj_kernel(nparts, steps, has_next, *refs):
    a_refs = refs[:nparts]
    w_ref, x_ref, pw_ref, gate_ref = refs[nparts:nparts + 4]
    pos = nparts + 4
    if has_next:
        nw_ref, sc_ref, sh_ref = refs[pos:pos + 3]
        pos += 3
    xo_ref = refs[pos]
    pos += 1
    if has_next:
        ho_ref = refs[pos]
        pos += 1
    acc = refs[pos]
    kk = pl.program_id(1)
    nk = sum(steps)

    @pl.when(kk == 0)
    def _():
        acc[...] = jnp.zeros_like(acc)

    start = 0
    for p in range(nparts):
        @pl.when((kk >= start) & (kk < start + steps[p]))
        def _(p=p):
            acc[...] += _dot(a_refs[p][...], w_ref[...])
        start += steps[p]

    @pl.when(kk == nk - 1)
    def _():
        xn = x_ref[...] + gate_ref[0] * (_rms(acc[...]) * pw_ref[...])
        xo_ref[...] = xn
        if has_next:
            ho_ref[...] = ((_rms(xn) * nw_ref[...]) * (1.0 + sc_ref[0]) + sh_ref[0]).astype(BF16)


def _outproj(parts, w, x2, post_w, gate, nxt, seq, tm=512, tk=512):
    n, d = x2.shape
    per_b = seq // tm
    steps = tuple(p.shape[1] // tk for p in parts)
    starts = [sum(steps[:i]) for i in range(len(parts))]
    nk = sum(steps)
    has_next = nxt is not None
    vec = pl.BlockSpec((1, 1, d), lambda i, k: (i // per_b, 0, 0))
    row = pl.BlockSpec((1, d), lambda i, k: (0, 0))
    in_specs = [pl.BlockSpec((tm, tk), functools.partial(
        lambda i, k, s0, ns: (i, jnp.clip(k - s0, 0, ns - 1)), s0=starts[p], ns=steps[p]))
        for p in range(len(parts))]
    in_specs += [pl.BlockSpec((tk, d), lambda i, k: (k, 0)),
                 pl.BlockSpec((tm, d), lambda i, k: (i, 0)), row, vec]
    args = list(parts) + [w, x2, post_w.reshape(1, d), gate]
    out_shape = [jax.ShapeDtypeStruct((n, d), F32)]
    out_specs = [pl.BlockSpec((tm, d), lambda i, k: (i, 0))]
    if has_next:
        nw, sc, sh = nxt
        in_specs += [row, vec, vec]
        args += [nw.reshape(1, d), sc, sh]
        out_shape.append(jax.ShapeDtypeStruct((n, d), BF16))
        out_specs.append(pl.BlockSpec((tm, d), lambda i, k: (i, 0)))
    res = pl.pallas_call(
        functools.partial(_outproj_kernel, len(parts), steps, has_next),
        grid=(n // tm, nk),
        in_specs=in_specs,
        out_specs=out_specs,
        out_shape=out_shape,
        scratch_shapes=[pltpu.VMEM((tm, d), F32)],
        compiler_params=_cparams(("arbitrary", "arbitrary"), 48),
        name="outproj",
    )(*args)
    return (res[0], res[1]) if has_next else (res[0], None)


def _ffn_in_kernel(seq, h_ref, wa_ref, wg_ref, cwa_ref, cwg_ref, cba_ref, cbg_ref, o_ref, ea, eg):
    tm = h_ref.shape[0]
    H = CONV_HALO
    m = pl.program_id(1)

    @pl.when((m * tm) % seq == 0)
    def _():
        ea[0:H, :] = jnp.zeros((H, ea.shape[1]), F32)
        eg[0:H, :] = jnp.zeros((H, eg.shape[1]), F32)

    h = h_ref[...]

    def conv(w_ref, cw_ref, cb_ref, ext):
        u = _dot(h, w_ref[...])
        ext[H:H + tm, :] = u
        y = (cw_ref[0:1, :] * ext[H - 2:H - 2 + tm, :] + cw_ref[1:2, :] * ext[H - 1:H - 1 + tm, :]
             + cw_ref[2:3, :] * u + cb_ref[...])
        ext[0:H, :] = ext[tm:tm + H, :]
        return y

    a = conv(wa_ref, cwa_ref, cba_ref, ea)
    g = conv(wg_ref, cwg_ref, cbg_ref, eg)
    o_ref[...] = (a * jax.nn.gelu(g, approximate=True)).astype(BF16)


def _ffn_in(h, w, conv_w, conv_b, seq, tm=1024, tn=512):
    n, d = h.shape
    nj = D_FF // tn
    return pl.pallas_call(
        functools.partial(_ffn_in_kernel, seq),
        grid=(nj, n // tm),
        in_specs=[pl.BlockSpec((tm, d), lambda j, m: (m, 0)),
                  pl.BlockSpec((d, tn), lambda j, m: (0, j)),
                  pl.BlockSpec((d, tn), lambda j, m: (0, nj + j)),
                  pl.BlockSpec((CONV_W, tn), lambda j, m: (0, j)),
                  pl.BlockSpec((CONV_W, tn), lambda j, m: (0, nj + j)),
                  pl.BlockSpec((1, tn), lambda j, m: (0, j)),
                  pl.BlockSpec((1, tn), lambda j, m: (0, nj + j))],
        out_specs=pl.BlockSpec((tm, tn), lambda j, m: (m, j)),
        out_shape=jax.ShapeDtypeStruct((n, D_FF), BF16),
        scratch_shapes=[pltpu.VMEM((tm + CONV_HALO, tn), F32), pltpu.VMEM((tm + CONV_HALO, tn), F32)],
        compiler_params=_cparams(("arbitrary", "arbitrary"), 48),
        name="ffn_in",
    )(h, w, w, conv_w, conv_w, conv_b, conv_b)


def _even_weights(w_in, m_gate_b, fox_f_b):
    d = w_in.shape[0]
    qk, vv, fw = M_HEADS * M_DQK, M_HEADS * M_DV, F_HEADS * F_DH
    o = 0
    mq = w_in[:, o:o + qk]; o += qk
    mk = w_in[:, o:o + qk]; o += qk
    mv = w_in[:, o:o + vv]; o += vv
    mo = w_in[:, o:o + vv]; o += vv
    mg = w_in[:, o:o + 2 * M_HEADS]; o += 2 * M_HEADS
    fq = w_in[:, o:o + fw]; o += fw
    fk = w_in[:, o:o + fw]; o += fw
    fv = w_in[:, o:o + fw]; o += fw
    ff = w_in[:, o:o + F_HEADS]
    main = jnp.concatenate([mq * (M_DQK ** -0.5), mk, mv, mo, fq * (F_DH ** -0.5), fk, fv], axis=1).astype(BF16)
    gates = jnp.concatenate([mg, ff], axis=1)
    wg = jnp.pad(gates, ((0, 0), (0, GATE_W - N_GATE_ROWS))).astype(BF16)
    wgt = gates.T.astype(BF16)
    bias = jnp.concatenate([m_gate_b, fox_f_b])
    bias_col = jnp.pad(bias, (0, GATE_W - N_GATE_ROWS)).reshape(1, GATE_W)
    bias_row = jnp.broadcast_to(bias[:, None], (N_GATE_ROWS, LANE))
    return main, wg, wgt, bias_col, bias_row


def _odd_weights(w_in, a_w2):
    main = w_in[:, :MIX_W].astype(BF16)
    ga = w_in[:, MIX_W:MIX_W + G_RANK]
    wg = jnp.pad(ga, ((0, 0), (0, GATE_W - G_RANK))).astype(BF16)
    wgt = jnp.zeros((N_GATE_ROWS, w_in.shape[0]), BF16)
    aw2p = jnp.pad(a_w2, ((0, GATE_W - G_RANK), (0, 0))).astype(BF16)
    return main, wg, wgt, aw2p


def kernel(x, c, ada_w, ada_b, norm_w, even_in_w, m_gate_b, m_norm_w, fox_f_b, even_out_w, odd_in_w,
           gla_a_w2, gla_a_b, gla_norm_w, odd_out_w, ffn_in_w, ffn_conv_w, ffn_conv_b, ffn_out_w):
    bsz, seq, d = x.shape
    n = bsz * seq
    mod = _ada_mod(c, ada_w, ada_b)
    mod = mod.reshape(DEPTH, bsz, 6, 1, d)

    def mvec(layer, idx):
        return mod[layer, :, idx]

    x2 = x.reshape(n, d)
    h = _prenorm(x2, norm_w[0, 0], mvec(0, 1), mvec(0, 0), seq)
    for layer in range(DEPTH):
        j = layer // 2
        if layer % 2 == 0:
            main, wg, wgt, bias_col, bias_row = _even_weights(even_in_w[j], m_gate_b[j], fox_f_b[j])
            u, gcol, grow = _inproj(h, main, wg, wgt, True)
            u3 = u.reshape(bsz, seq, MIX_W)
            ym, fcol, frow = _mlstm(u3, gcol.reshape(bsz, seq, GATE_W), grow, bias_col, bias_row,
                                    m_norm_w[j].reshape(1, -1))
            yf = _fox(u3, fcol, frow)
            parts = [ym.reshape(n, -1), yf.reshape(n, -1)]
            w_out = even_out_w[j].astype(BF16)
        else:
            main, wg, wgt, aw2p = _odd_weights(odd_in_w[j], gla_a_w2[j])
            u, gcol = _inproj(h, main, wg, wgt, False)
            yo = _gla(u.reshape(bsz, seq, MIX_W), gcol.reshape(bsz, seq, GATE_W), aw2p,
                      gla_a_b[j].reshape(1, -1), gla_norm_w[j].reshape(1, -1))
            parts = [yo.reshape(n, -1)]
            w_out = odd_out_w[j].astype(BF16)
        x2, h = _outproj(parts, w_out, x2, norm_w[layer, 1], mvec(layer, 2),
                         (norm_w[layer, 2], mvec(layer, 4), mvec(layer, 3)), seq)
        a = _ffn_in(h, ffn_in_w[layer].astype(BF16), ffn_conv_w[layer].reshape(CONV_W, 2 * D_FF),
                    ffn_conv_b[layer].reshape(1, 2 * D_FF), seq)
        nxt = None
        if layer + 1 < DEPTH:
            nxt = (norm_w[layer + 1, 0], mvec(layer + 1, 1), mvec(layer + 1, 0))
        x2, h = _outproj([a], ffn_out_w[layer].astype(BF16), x2, norm_w[layer, 3], mvec(layer, 5), nxt, seq)
    return x2.reshape(bsz, seq, d)
```

```python
import functools

import jax
import jax.numpy as jnp
from jax import lax
from jax.experimental import pallas as pl
from jax.experimental.pallas import tpu as pltpu

F32 = jnp.float32
BF16 = jnp.bfloat16

LANE = 128
SUBLANE = 8
VMEM_BYTES_V7X = 64 * 1024 * 1024

D_MODEL = 2048
DEPTH = 4
EPS = 1e-6
M_HEADS, M_DQK, M_DV = 4, 128, 256
F_HEADS, F_DH = 8, 128
G_HEADS, G_DK, G_DV = 4, 256, 512
G_RANK = 16
G_TAU = 16.0
D_FF = 5632
CONV_W = 3

MIX_W = 6144
GATE_W = LANE
N_GATE_ROWS = 16
M_CHUNK = 256
G_BLOCK = 256
G_CHUNK = 64
G_SUB = 16
FOX_T = 512
FOX_COLS = 256
LOG2E = 1.4426950408889634
CONV_HALO = SUBLANE
FFN_ROWS = 256
MIX_OUT_TK = 1024
FFN_OUT_TK = D_FF // 4


def _cparams(sem, vmem_mb):
    return pltpu.CompilerParams(dimension_semantics=sem, vmem_limit_bytes=vmem_mb * 1024 * 1024)


def _dot(a, b):
    return jnp.dot(a, b, preferred_element_type=F32)


def _dot_nt(a, b):
    return lax.dot_general(a, b, (((1,), (1,)), ((), ())), preferred_element_type=F32)


def _dot_tn(a, b):
    return lax.dot_general(a, b, (((0,), (0,)), ((), ())), preferred_element_type=F32)


def _split3(x):
    hi = x.astype(BF16)
    r = x - hi.astype(F32)
    mid = r.astype(BF16)
    lo = (r - mid.astype(F32)).astype(BF16)
    return hi, mid, lo


def _cumsum_rows(tri_lower, x):
    hi, mid, lo = _split3(x)
    return _dot(tri_lower, hi) + _dot(tri_lower, mid) + _dot(tri_lower, lo)


def _cumsum_lanes(x, tri_upper):
    hi, mid, lo = _split3(x)
    return _dot(hi, tri_upper) + _dot(mid, tri_upper) + _dot(lo, tri_upper)


def _tri(n, lower):
    r = lax.broadcasted_iota(jnp.int32, (n, n), 0)
    c = lax.broadcasted_iota(jnp.int32, (n, n), 1)
    return jnp.where((c <= r) if lower else (r <= c), 1.0, 0.0).astype(BF16)


def _log_sigmoid(z):
    return jnp.minimum(z, 0.0) - jnp.log1p(jnp.exp(-jnp.abs(z)))


def _rms(x):
    return x * lax.rsqrt(jnp.mean(x * x, axis=-1, keepdims=True) + EPS)


def _ada_kernel(ct_ref, w_ref, b_ref, o_ref):
    d, bsz = ct_ref.shape
    tn = w_ref.shape[2]
    rows_per_step = 512

    def body(r, accs):
        rows = pl.ds(pl.multiple_of(r * rows_per_step, rows_per_step), rows_per_step)
        ct = ct_ref[rows, :]
        cond = ct * jax.nn.sigmoid(ct)
        w = w_ref[0, rows, :]
        return tuple(accs[b] + jnp.sum(cond[:, b:b + 1] * w, axis=0, keepdims=True) for b in range(bsz))

    accs = lax.fori_loop(0, d // rows_per_step, body, tuple(jnp.zeros((1, tn), F32) for _ in range(bsz)))
    for b in range(bsz):
        o_ref[0, b:b + 1, :] = accs[b] + b_ref[0]


def _ada_mod(c, ada_w, ada_b):
    bsz, d = c.shape
    depth, _, n = ada_w.shape
    tn = 1024
    return pl.pallas_call(
        _ada_kernel,
        grid=(depth, n // tn),
        in_specs=[pl.BlockSpec((d, bsz), lambda l, j: (0, 0)),
                  pl.BlockSpec((1, d, tn), lambda l, j: (l, 0, j)),
                  pl.BlockSpec((1, 1, tn), lambda l, j: (l, 0, j))],
        out_specs=pl.BlockSpec((1, bsz, tn), lambda l, j: (l, 0, j)),
        out_shape=jax.ShapeDtypeStruct((depth, bsz, n), F32),
        compiler_params=_cparams(("arbitrary", "arbitrary"), 40),
        name="ada_mod",
    )(c.T, ada_w, ada_b.reshape(depth, 1, n))


def _prenorm_kernel(x_ref, w_ref, sc_ref, sh_ref, h_ref):
    y = _rms(x_ref[...]) * w_ref[...]
    h_ref[...] = (y * (1.0 + sc_ref[0]) + sh_ref[0]).astype(BF16)


def _prenorm(x2, w, sc, sh, seq, tm=512):
    n, d = x2.shape
    per_b = seq // tm
    vec = pl.BlockSpec((1, 1, d), lambda i: (i // per_b, 0, 0))
    return pl.pallas_call(
        _prenorm_kernel,
        grid=(n // tm,),
        in_specs=[pl.BlockSpec((tm, d), lambda i: (i, 0)),
                  pl.BlockSpec((1, d), lambda i: (0, 0)), vec, vec],
        out_specs=pl.BlockSpec((tm, d), lambda i: (i, 0)),
        out_shape=jax.ShapeDtypeStruct((n, d), BF16),
        compiler_params=_cparams(("arbitrary",), 32),
        name="prenorm",
    )(x2, w.reshape(1, d), sc, sh)


def _inproj_kernel(h_ref, w_ref, wg_ref, wgt_ref, o_ref, gcol_ref, *grow_ref):
    h = h_ref[...]
    o_ref[...] = _dot(h, w_ref[...]).astype(BF16)

    @pl.when(pl.program_id(1) == 0)
    def _():
        gcol_ref[...] = _dot(h, wg_ref[...])
        if grow_ref:
            grow_ref[0][...] = _dot_nt(wgt_ref[...], h)


def _inproj(h, w, wg, wgt, want_rows, tm=1024, tn=512):
    n, d = h.shape
    ncol = w.shape[1]
    out_shape = [jax.ShapeDtypeStruct((n, ncol), BF16), jax.ShapeDtypeStruct((n, GATE_W), F32)]
    out_specs = [pl.BlockSpec((tm, tn), lambda i, j: (i, j)),
                 pl.BlockSpec((tm, GATE_W), lambda i, j: (i, 0))]
    if want_rows:
        out_shape.append(jax.ShapeDtypeStruct((N_GATE_ROWS, n), F32))
        out_specs.append(pl.BlockSpec((N_GATE_ROWS, tm), lambda i, j: (0, i)))
    return pl.pallas_call(
        _inproj_kernel,
        grid=(n // tm, ncol // tn),
        in_specs=[pl.BlockSpec((tm, d), lambda i, j: (i, 0)),
                  pl.BlockSpec((d, tn), lambda i, j: (0, j)),
                  pl.BlockSpec((d, GATE_W), lambda i, j: (0, 0)),
                  pl.BlockSpec((N_GATE_ROWS, d), lambda i, j: (0, 0))],
        out_specs=out_specs,
        out_shape=out_shape,
        compiler_params=_cparams(("arbitrary", "arbitrary"), 40),
        name="inproj",
    )(h, w, wg, wgt)


def _mlstm_kernel(q_ref, k_ref, v_ref, og_ref, gcol_ref, grow_ref, bcol_ref, brow_ref, nw_ref,
                  y_ref, fcol_ref, frow_ref, c_sc, n_sc, m_sc, carry_col, carry_row):
    L = M_CHUNK

    @pl.when(pl.program_id(1) == 0)
    def _():
        c_sc[...] = jnp.zeros_like(c_sc)
        n_sc[...] = jnp.zeros_like(n_sc)
        m_sc[...] = jnp.zeros_like(m_sc)
        carry_col[...] = jnp.zeros_like(carry_col)
        carry_row[...] = jnp.zeros_like(carry_row)

    gcol = gcol_ref[0] + bcol_ref[...]
    grow = grow_ref[...] + brow_ref[:, 0:1]
    tri_l = _tri(L, True)
    tri_u = _tri(L, False)
    cum_col = _cumsum_rows(tri_l, _log_sigmoid(gcol))
    cum_row = _cumsum_lanes(_log_sigmoid(grow), tri_u)

    fcol = cum_col + carry_col[0:1, :]
    frow = cum_row + carry_row[:, 0:1]
    fcol_ref[0] = fcol
    frow_ref[0] = frow[M_HEADS * 2:, :]
    carry_col[...] = jnp.broadcast_to(fcol[L - 1:L, :], carry_col.shape)
    carry_row[...] = jnp.broadcast_to(frow[:, L - 1:L], carry_row.shape)

    causal = lax.broadcasted_iota(jnp.int32, (L, L), 0) >= lax.broadcasted_iota(jnp.int32, (L, L), 1)
    for hd in range(M_HEADS):
        q = q_ref[0, :, hd * M_DQK:(hd + 1) * M_DQK]
        k = k_ref[0, :, hd * M_DQK:(hd + 1) * M_DQK]
        v = v_ref[0, :, hd * M_DV:(hd + 1) * M_DV]
        ig_c = gcol[:, hd:hd + 1]
        ig_r = grow[hd:hd + 1, :]
        cum_c = cum_col[:, M_HEADS + hd:M_HEADS + hd + 1]
        cum_r = cum_row[M_HEADS + hd:M_HEADS + hd + 1, :]
        m_prev = m_sc[hd, 0:1, 0:1]
        c_st = c_sc[hd]
        n_st = n_sc[hd, 0:1, :]

        dmat = jnp.where(causal, cum_c - cum_r + ig_r, -jnp.inf)
        inter = cum_c + m_prev
        m_t = jnp.maximum(inter, jnp.max(dmat, axis=-1, keepdims=True))
        s = _dot_nt(q, k) * jnp.exp(dmat - m_t)
        g = jnp.exp(inter - m_t)
        qf = q.astype(F32)
        num = _dot(s.astype(BF16), v) + g * _dot(q, c_st.astype(BF16))
        den = jnp.sum(s, axis=-1, keepdims=True) + g * jnp.sum(qf * n_st, axis=-1, keepdims=True)
        h_out = num / jnp.maximum(jnp.abs(den), jnp.exp(-m_t))

        last = cum_c[L - 1:L, :]
        dec = last - cum_c + ig_c
        m_new = jnp.maximum(last + m_prev, jnp.max(dec, axis=0, keepdims=True))
        ws = jnp.exp(dec - m_new)
        gs = jnp.exp(last + m_prev - m_new)
        kw = k.astype(F32) * ws
        c_sc[hd] = gs * c_st + _dot_tn(kw.astype(BF16), v)
        n_sc[hd] = jnp.broadcast_to(gs * n_st + jnp.sum(kw, axis=0, keepdims=True), n_sc.shape[1:])
        m_sc[hd] = jnp.broadcast_to(m_new, m_sc.shape[1:])

        og = og_ref[0, :, hd * M_DV:(hd + 1) * M_DV].astype(F32)
        hn = _rms(h_out) * nw_ref[:, hd * M_DV:(hd + 1) * M_DV]
        y_ref[0, :, hd * M_DV:(hd + 1) * M_DV] = (hn * jax.nn.sigmoid(og)).astype(BF16)


def _mlstm(u, gcol, grow, bias_col, bias_row, norm_w):
    bsz, seq, _ = u.shape
    L = M_CHUNK
    nc = seq // L
    qk_w = M_HEADS * M_DQK
    v_w = M_HEADS * M_DV
    return pl.pallas_call(
        _mlstm_kernel,
        grid=(bsz, nc),
        in_specs=[pl.BlockSpec((1, L, qk_w), lambda b, c: (b, c, 0)),
                  pl.BlockSpec((1, L, qk_w), lambda b, c: (b, c, 1)),
                  pl.BlockSpec((1, L, v_w), lambda b, c: (b, c, 1)),
                  pl.BlockSpec((1, L, v_w), lambda b, c: (b, c, 2)),
                  pl.BlockSpec((1, L, GATE_W), lambda b, c: (b, c, 0)),
                  pl.BlockSpec((N_GATE_ROWS, L), lambda b, c: (0, b * nc + c)),
                  pl.BlockSpec((1, GATE_W), lambda b, c: (0, 0)),
                  pl.BlockSpec((N_GATE_ROWS, LANE), lambda b, c: (0, 0)),
                  pl.BlockSpec((1, v_w), lambda b, c: (0, 0))],
        out_specs=[pl.BlockSpec((1, L, v_w), lambda b, c: (b, c, 0)),
                   pl.BlockSpec((1, L, GATE_W), lambda b, c: (b, c, 0)),
                   pl.BlockSpec((1, F_HEADS, L), lambda b, c: (b, 0, c))],
        out_shape=[jax.ShapeDtypeStruct((bsz, seq, v_w), BF16),
                   jax.ShapeDtypeStruct((bsz, seq, GATE_W), F32),
                   jax.ShapeDtypeStruct((bsz, F_HEADS, seq), F32)],
        scratch_shapes=[pltpu.VMEM((M_HEADS, M_DQK, M_DV), F32),
                        pltpu.VMEM((M_HEADS, SUBLANE, M_DQK), F32),
                        pltpu.VMEM((M_HEADS, SUBLANE, LANE), F32),
                        pltpu.VMEM((SUBLANE, GATE_W), F32),
                        pltpu.VMEM((N_GATE_ROWS, LANE), F32)],
        compiler_params=_cparams(("arbitrary", "arbitrary"), 40),
        name="mlstm",
    )(u, u, u, u, gcol, grow, bias_col, bias_row, norm_w)


def _fox_kernel(q_ref, k_ref, v_ref, fcol_ref, frow_ref, o_ref, k_sc, vt_sc, ck_sc, m_sc, l_sc, acc_sc,
                za_sc, zb_sc):
    tq, C = FOX_T, FOX_COLS
    hd = pl.program_id(1)
    qi = pl.program_id(2)
    q = q_ref[0]
    here = pl.ds(pl.multiple_of(qi * tq, tq), tq)

    k_sc[here, :] = k_ref[0]
    vt_sc[:, here] = v_ref[0].astype(F32).T.astype(BF16)
    lane = lax.broadcasted_iota(jnp.int32, (tq, GATE_W), 1)
    ck = jnp.sum(jnp.where(lane == 2 * M_HEADS + hd, fcol_ref[0], 0.0), axis=-1, keepdims=True)
    ck_sc[here, :] = jnp.broadcast_to(ck * LOG2E, (tq, LANE))
    cq = frow_ref[0, pl.ds(hd, 1), here] * LOG2E

    m_sc[...] = jnp.full_like(m_sc, -jnp.inf)
    l_sc[...] = jnp.zeros_like(l_sc)
    acc_sc[...] = jnp.zeros_like(acc_sc)

    groups = tq // C

    def scores(j, g):
        off = pl.multiple_of(j * tq, tq)
        ckr = ck_sc[pl.ds(off, tq), :]
        return (_dot_nt(k_sc[pl.ds(off, tq), :], q[g * C:(g + 1) * C])
                - jnp.concatenate([ckr] * (C // LANE), axis=1))

    def update(j, g, z):
        off = pl.multiple_of(j * tq, tq)
        cs = slice(g * C, (g + 1) * C)
        nk = z.shape[0]
        m_old = m_sc[:, cs]
        m_new = jnp.maximum(m_old, jnp.max(z, axis=0, keepdims=True) + cq[:, cs])
        p = jnp.exp2(z - (m_new - cq[:, cs]))
        alpha = jnp.exp2(m_old - m_new)
        l_sc[:, cs] = alpha * l_sc[:, cs] + jnp.sum(p, axis=0, keepdims=True)
        acc_sc[:, cs] = alpha * acc_sc[:, cs] + _dot(vt_sc[:, pl.ds(off, nk)], p.astype(BF16))
        m_sc[:, cs] = m_new

    def step(j, z_cur, z_nxt):
        for g in range(groups):
            cs = slice(g * C, (g + 1) * C)
            z_nxt[:, cs] = scores(j + 1, g)
            update(j, g, z_cur[:, cs])

    def diagonal(z_ref):
        for g in range(groups):
            nk = (g + 1) * C
            r = lax.broadcasted_iota(jnp.int32, (nk, C), 0)
            c = lax.broadcasted_iota(jnp.int32, (nk, C), 1) + g * C
            update(qi, g, jnp.where(r <= c, z_ref[0:nk, g * C:(g + 1) * C], -jnp.inf))

    for g in range(groups):
        za_sc[:, g * C:(g + 1) * C] = scores(0, g)

    def pair(i, carry):
        step(2 * i, za_sc, zb_sc)
        step(2 * i + 1, zb_sc, za_sc)
        return carry

    lax.fori_loop(0, qi // 2, pair, 0)
    odd = qi % 2 == 1

    @pl.when(odd)
    def _():
        step(qi - 1, za_sc, zb_sc)
        diagonal(zb_sc)

    @pl.when(jnp.logical_not(odd))
    def _():
        diagonal(za_sc)

    o_ref[0] = (acc_sc[...] / l_sc[...]).T.astype(BF16)


def _fox(u, fcol, frow):
    bsz, seq, _ = u.shape
    tq = FOX_T
    q_blk = (M_HEADS * (2 * M_DQK + 2 * M_DV)) // F_DH
    k_blk = q_blk + F_HEADS
    v_blk = k_blk + F_HEADS
    return pl.pallas_call(
        _fox_kernel,
        grid=(bsz, F_HEADS, seq // tq),
        in_specs=[pl.BlockSpec((1, tq, F_DH), lambda b, h, i: (b, i, q_blk + h)),
                  pl.BlockSpec((1, tq, F_DH), lambda b, h, i: (b, i, k_blk + h)),
                  pl.BlockSpec((1, tq, F_DH), lambda b, h, i: (b, i, v_blk + h)),
                  pl.BlockSpec((1, tq, GATE_W), lambda b, h, i: (b, i, 0)),
                  pl.BlockSpec((1, F_HEADS, seq), lambda b, h, i: (b, 0, 0))],
        out_specs=pl.BlockSpec((1, tq, F_DH), lambda b, h, i: (b, i, h)),
        out_shape=jax.ShapeDtypeStruct((bsz, seq, F_HEADS * F_DH), BF16),
        scratch_shapes=[pltpu.VMEM((seq, F_DH), BF16), pltpu.VMEM((F_DH, seq), BF16),
                        pltpu.VMEM((seq, LANE), F32),
                        pltpu.VMEM((1, tq), F32), pltpu.VMEM((1, tq), F32), pltpu.VMEM((F_DH, tq), F32),
                        pltpu.VMEM((tq, tq), F32), pltpu.VMEM((tq, tq), F32)],
        compiler_params=_cparams(("arbitrary", "arbitrary", "arbitrary"), 48),
        name="fox",
    )(u, u, u, fcol, frow)


def _gla_kernel(q_ref, k_ref, v_ref, og_ref, ga_ref, aw_ref, ab_ref, nw_ref, o_ref, st_sc):
    L, S = G_CHUNK, G_SUB
    nsub = L // S

    @pl.when(pl.program_id(2) == 0)
    def _():
        st_sc[...] = jnp.zeros_like(st_sc)

    tri_l = _tri(L, True)
    row = lax.broadcasted_iota(jnp.int32, (L, 1), 0)
    lane_l = lax.broadcasted_iota(jnp.int32, (S, L), 1)
    row_s = lax.broadcasted_iota(jnp.int32, (S, L), 0)

    def chunk(cc, carry):
        r0 = pl.multiple_of(cc * L, L)
        rows = pl.ds(r0, L)
        q = q_ref[0, rows, :].astype(F32) * (G_DK ** -0.5)
        k = k_ref[0, rows, :].astype(F32)
        v = v_ref[0, rows, :]
        z = _dot(ga_ref[0, rows, :].astype(BF16), aw_ref[...]) + ab_ref[...]
        la = _log_sigmoid(z) * (1.0 / G_TAU)
        c = _cumsum_rows(tri_l, la)
        c_last = c[L - 1:L, :]
        st = st_sc[...]

        o = _dot_nt((q * jnp.exp(c)).astype(BF16), st.astype(BF16))

        ref_c = c
        for i in range(1, nsub):
            ref_c = jnp.where(row >= i * S, c[i * S - 1:i * S, :], ref_c)
        qt = (q * jnp.exp(c - ref_c)).astype(BF16)
        a_rows = [jnp.zeros((S, L), F32)]
        for i in range(1, nsub):
            e = jnp.where(row < i * S, c[i * S - 1:i * S, :] - c, -jnp.inf)
            kt = (k * jnp.exp(e)).astype(BF16)
            a_rows.append(_dot_nt(qt[i * S:(i + 1) * S, :], kt))

        for i in range(nsub):
            qi = q[i * S:(i + 1) * S, :]
            ci = c[i * S:(i + 1) * S, :]
            a_d = a_rows[i]
            for sl in range(S):
                s_abs = i * S + sl
                e = jnp.minimum(ci - c[s_abs:s_abs + 1, :], 0.0)
                col = jnp.sum(qi * (k[s_abs:s_abs + 1, :] * jnp.exp(e)), axis=-1, keepdims=True)
                a_d = jnp.where((lane_l == s_abs) & (row_s >= sl), col, a_d)
            a_rows[i] = a_d
        a = jnp.concatenate(a_rows, axis=0)
        o = o + _dot(a.astype(BF16), v)

        kd = (k * jnp.exp(c_last - c)).astype(BF16)
        st_sc[...] = st * jnp.exp(c_last) + _dot_tn(v, kd)

        og = og_ref[0, rows, :].astype(F32)
        on = _rms(o) * nw_ref[...]
        o_ref[0, rows, :] = (on * (og * jax.nn.sigmoid(og))).astype(BF16)
        return carry

    lax.fori_loop(0, G_BLOCK // L, chunk, 0)


def _gla(u, gcol, aw2p, ab, norm_w):
    bsz, seq, _ = u.shape
    blk = G_BLOCK
    kq = (G_HEADS * G_DK) // G_DK
    kv = (2 * G_HEADS * G_DK) // G_DV
    return pl.pallas_call(
        _gla_kernel,
        grid=(bsz, G_HEADS, seq // blk),
        in_specs=[pl.BlockSpec((1, blk, G_DK), lambda b, h, c: (b, c, h)),
                  pl.BlockSpec((1, blk, G_DK), lambda b, h, c: (b, c, kq + h)),
                  pl.BlockSpec((1, blk, G_DV), lambda b, h, c: (b, c, kv + h)),
                  pl.BlockSpec((1, blk, G_DV), lambda b, h, c: (b, c, kv + G_HEADS + h)),
                  pl.BlockSpec((1, blk, GATE_W), lambda b, h, c: (b, c, 0)),
                  pl.BlockSpec((GATE_W, G_DK), lambda b, h, c: (0, h)),
                  pl.BlockSpec((1, G_DK), lambda b, h, c: (0, h)),
                  pl.BlockSpec((1, G_DV), lambda b, h, c: (0, h))],
        out_specs=pl.BlockSpec((1, blk, G_DV), lambda b, h, c: (b, c, h)),
        out_shape=jax.ShapeDtypeStruct((bsz, seq, G_HEADS * G_DV), BF16),
        scratch_shapes=[pltpu.VMEM((G_DV, G_DK), F32)],
        compiler_params=_cparams(("arbitrary", "arbitrary", "arbitrary"), 40),
        name="gla",
    )(u, u, u, u, gcol, aw2p, ab, norm_w)


def _outproj_kernel(nparts, steps, has_next, *refs):
    a_refs = refs[:nparts]
    w_ref, x_ref, pw_ref, gate_ref = refs[nparts:nparts + 4]
    pos = nparts + 4
    if has_next:
        nw_ref, sc_ref, sh_ref = refs[pos:pos + 3]
        pos += 3
    xo_ref = refs[pos]
    pos += 1
    if has_next:
        ho_ref = refs[pos]
        pos += 1
    acc = refs[pos]
    kk = pl.program_id(1)
    nk = sum(steps)

    @pl.when(kk == 0)
    def _():
        acc[...] = jnp.zeros_like(acc)

    start = 0
    for p in range(nparts):
        @pl.when((kk >= start) & (kk < start + steps[p]))
        def _(p=p):
            acc[...] += _dot(a_refs[p][...], w_ref[...])
        start += steps[p]

    @pl.when(kk == nk - 1)
    def _():
        xn = x_ref[...] + gate_ref[0] * (_rms(acc[...]) * pw_ref[...])
        xo_ref[...] = xn
        if has_next:
            ho_ref[...] = ((_rms(xn) * nw_ref[...]) * (1.0 + sc_ref[0]) + sh_ref[0]).astype(BF16)


def _outproj(parts, w, x2, post_w, gate, nxt, seq, tm=512, tk=512):
    n, d = x2.shape
    per_b = seq // tm
    steps = tuple(p.shape[1] // tk for p in parts)
    starts = [sum(steps[:i]) for i in range(len(parts))]
    nk = sum(steps)
    has_next = nxt is not None
    vec = pl.BlockSpec((1, 1, d), lambda i, k: (i // per_b, 0, 0))
    row = pl.BlockSpec((1, d), lambda i, k: (0, 0))
    in_specs = [pl.BlockSpec((tm, tk), functools.partial(
        lambda i, k, s0, ns: (i, jnp.clip(k - s0, 0, ns - 1)), s0=starts[p], ns=steps[p]))
        for p in range(len(parts))]
    in_specs += [pl.BlockSpec((tk, d), lambda i, k: (k, 0)),
                 pl.BlockSpec((tm, d), lambda i, k: (i, 0)), row, vec]
    args = list(parts) + [w, x2, post_w.reshape(1, d), gate]
    out_shape = [jax.ShapeDtypeStruct((n, d), F32)]
    out_specs = [pl.BlockSpec((tm, d), lambda i, k: (i, 0))]
    if has_next:
        nw, sc, sh = nxt
        in_specs += [row, vec, vec]
        args += [nw.reshape(1, d), sc, sh]
        out_shape.append(jax.ShapeDtypeStruct((n, d), BF16))
        out_specs.append(pl.BlockSpec((tm, d), lambda i, k: (i, 0)))
    res = pl.pallas_call(
        functools.partial(_outproj_kernel, len(parts), steps, has_next),
        grid=(n // tm, nk),
        in_specs=in_specs,
        out_specs=out_specs,
        out_shape=out_shape,
        scratch_shapes=[pltpu.VMEM((tm, d), F32)],
        compiler_params=_cparams(("arbitrary", "arbitrary"), 48),
        name="outproj",
    )(*args)
    return (res[0], res[1]) if has_next else (res[0], None)


def _ffn_in_kernel(seq, h_ref, wa_ref, wg_ref, cwa_ref, cwg_ref, cba_ref, cbg_ref, o_ref, ea, eg):
    tm = h_ref.shape[0]
    H = CONV_HALO
    m = pl.program_id(1)

    @pl.when((m * tm) % seq == 0)
    def _():
        ea[0:H, :] = jnp.zeros((H, ea.shape[1]), F32)
        eg[0:H, :] = jnp.zeros((H, eg.shape[1]), F32)

    def conv(h, r0, w_ref, cw_ref, cb_ref, ext):
        u = _dot(h, w_ref[...])
        nr = u.shape[0]
        ext[H + r0:H + r0 + nr, :] = u
        return (cw_ref[0:1, :] * ext[H - 2 + r0:H - 2 + r0 + nr, :]
                + cw_ref[1:2, :] * ext[H - 1 + r0:H - 1 + r0 + nr, :] + cw_ref[2:3, :] * u + cb_ref[...])

    for r0 in range(0, tm, FFN_ROWS):
        h = h_ref[r0:r0 + FFN_ROWS, :]
        a = conv(h, r0, wa_ref, cwa_ref, cba_ref, ea)
        g = conv(h, r0, wg_ref, cwg_ref, cbg_ref, eg)
        o_ref[r0:r0 + FFN_ROWS, :] = (a * jax.nn.gelu(g, approximate=True)).astype(BF16)
    ea[0:H, :] = ea[tm:tm + H, :]
    eg[0:H, :] = eg[tm:tm + H, :]


def _ffn_in(h, w, conv_w, conv_b, seq, tm=1024, tn=512):
    n, d = h.shape
    nj = D_FF // tn
    return pl.pallas_call(
        functools.partial(_ffn_in_kernel, seq),
        grid=(nj, n // tm),
        in_specs=[pl.BlockSpec((tm, d), lambda j, m: (m, 0)),
                  pl.BlockSpec((d, tn), lambda j, m: (0, j)),
                  pl.BlockSpec((d, tn), lambda j, m: (0, nj + j)),
                  pl.BlockSpec((CONV_W, tn), lambda j, m: (0, j)),
                  pl.BlockSpec((CONV_W, tn), lambda j, m: (0, nj + j)),
                  pl.BlockSpec((1, tn), lambda j, m: (0, j)),
                  pl.BlockSpec((1, tn), lambda j, m: (0, nj + j))],
        out_specs=pl.BlockSpec((tm, tn), lambda j, m: (m, j)),
        out_shape=jax.ShapeDtypeStruct((n, D_FF), BF16),
        scratch_shapes=[pltpu.VMEM((tm + CONV_HALO, tn), F32), pltpu.VMEM((tm + CONV_HALO, tn), F32)],
        compiler_params=_cparams(("arbitrary", "arbitrary"), 48),
        name="ffn_in",
    )(h, w, w, conv_w, conv_w, conv_b, conv_b)


def _even_weights(w_in, m_gate_b, fox_f_b):
    d = w_in.shape[0]
    qk, vv, fw = M_HEADS * M_DQK, M_HEADS * M_DV, F_HEADS * F_DH
    o = 0
    mq = w_in[:, o:o + qk]; o += qk
    mk = w_in[:, o:o + qk]; o += qk
    mv = w_in[:, o:o + vv]; o += vv
    mo = w_in[:, o:o + vv]; o += vv
    mg = w_in[:, o:o + 2 * M_HEADS]; o += 2 * M_HEADS
    fq = w_in[:, o:o + fw]; o += fw
    fk = w_in[:, o:o + fw]; o += fw
    fv = w_in[:, o:o + fw]; o += fw
    ff = w_in[:, o:o + F_HEADS]
    main = jnp.concatenate([mq * (M_DQK ** -0.5), mk, mv, mo, fq * (F_DH ** -0.5 * LOG2E), fk, fv], axis=1).astype(BF16)
    gates = jnp.concatenate([mg, ff], axis=1)
    wg = jnp.pad(gates, ((0, 0), (0, GATE_W - N_GATE_ROWS))).astype(BF16)
    wgt = gates.T.astype(BF16)
    bias = jnp.concatenate([m_gate_b, fox_f_b])
    bias_col = jnp.pad(bias, (0, GATE_W - N_GATE_ROWS)).reshape(1, GATE_W)
    bias_row = jnp.broadcast_to(bias[:, None], (N_GATE_ROWS, LANE))
    return main, wg, wgt, bias_col, bias_row


def _odd_weights(w_in, a_w2):
    main = w_in[:, :MIX_W].astype(BF16)
    ga = w_in[:, MIX_W:MIX_W + G_RANK]
    wg = jnp.pad(ga, ((0, 0), (0, GATE_W - G_RANK))).astype(BF16)
    wgt = jnp.zeros((N_GATE_ROWS, w_in.shape[0]), BF16)
    aw2p = jnp.pad(a_w2, ((0, GATE_W - G_RANK), (0, 0))).astype(BF16)
    return main, wg, wgt, aw2p


def kernel(x, c, ada_w, ada_b, norm_w, even_in_w, m_gate_b, m_norm_w, fox_f_b, even_out_w, odd_in_w,
           gla_a_w2, gla_a_b, gla_norm_w, odd_out_w, ffn_in_w, ffn_conv_w, ffn_conv_b, ffn_out_w):
    bsz, seq, d = x.shape
    n = bsz * seq
    mod = _ada_mod(c, ada_w, ada_b)
    mod = mod.reshape(DEPTH, bsz, 6, 1, d)

    def mvec(layer, idx):
        return mod[layer, :, idx]

    x2 = x.reshape(n, d)
    h = _prenorm(x2, norm_w[0, 0], mvec(0, 1), mvec(0, 0), seq)
    for layer in range(DEPTH):
        j = layer // 2
        if layer % 2 == 0:
            main, wg, wgt, bias_col, bias_row = _even_weights(even_in_w[j], m_gate_b[j], fox_f_b[j])
            u, gcol, grow = _inproj(h, main, wg, wgt, True)
            u3 = u.reshape(bsz, seq, MIX_W)
            ym, fcol, frow = _mlstm(u3, gcol.reshape(bsz, seq, GATE_W), grow, bias_col, bias_row,
                                    m_norm_w[j].reshape(1, -1))
            yf = _fox(u3, fcol, frow)
            parts = [ym.reshape(n, -1), yf.reshape(n, -1)]
            w_out = even_out_w[j].astype(BF16)
        else:
            main, wg, wgt, aw2p = _odd_weights(odd_in_w[j], gla_a_w2[j])
            u, gcol = _inproj(h, main, wg, wgt, False)
            yo = _gla(u.reshape(bsz, seq, MIX_W), gcol.reshape(bsz, seq, GATE_W), aw2p,
                      gla_a_b[j].reshape(1, -1), gla_norm_w[j].reshape(1, -1))
            parts = [yo.reshape(n, -1)]
            w_out = odd_out_w[j].astype(BF16)
        x2, h = _outproj(parts, w_out, x2, norm_w[layer, 1], mvec(layer, 2),
                         (norm_w[layer, 2], mvec(layer, 4), mvec(layer, 3)), seq, tk=MIX_OUT_TK)
        a = _ffn_in(h, ffn_in_w[layer].astype(BF16), ffn_conv_w[layer].reshape(CONV_W, 2 * D_FF),
                    ffn_conv_b[layer].reshape(1, 2 * D_FF), seq)
        nxt = None
        if layer + 1 < DEPTH:
            nxt = (norm_w[layer + 1, 0], mvec(layer + 1, 1), mvec(layer + 1, 0))
        x2, h = _outproj([a], ffn_out_w[layer].astype(BF16), x2, norm_w[layer, 3], mvec(layer, 5), nxt, seq,
                         tk=FFN_OUT_TK)
    return x2.reshape(bsz, seq, d)
```

```python
import functools

import jax
import jax.numpy as jnp
from jax import lax
from jax.experimental import pallas as pl
from jax.experimental.pallas import tpu as pltpu

F32 = jnp.float32
BF16 = jnp.bfloat16

LANE = 128
SUBLANE = 8
VMEM_BYTES_V7X = 64 * 1024 * 1024

D_MODEL = 2048
DEPTH = 4
EPS = 1e-6
M_HEADS, M_DQK, M_DV = 4, 128, 256
F_HEADS, F_DH = 8, 128
G_HEADS, G_DK, G_DV = 4, 256, 512
G_RANK = 16
G_TAU = 16.0
D_FF = 5632
CONV_W = 3

MIX_W = 6144
GATE_W = LANE
N_GATE_ROWS = 16
M_CHUNK = 256
G_BLOCK = 256
G_CHUNK = 64
G_SUB = 16
G_HPB = 4
FOX_T = 512
FOX_COLS = 512
LOG2E = 1.4426950408889634
CONV_HALO = SUBLANE
GELU_C0 = 0.7978845608028654
GELU_C1 = 0.044715
FFN_OUT_TK = D_FF // 4


def _cparams(sem, vmem_mb):
    return pltpu.CompilerParams(dimension_semantics=sem, vmem_limit_bytes=vmem_mb * 1024 * 1024)


def _dot(a, b):
    return jnp.dot(a, b, preferred_element_type=F32)


def _dot_nt(a, b):
    return lax.dot_general(a, b, (((1,), (1,)), ((), ())), preferred_element_type=F32)


def _dot_tn(a, b):
    return lax.dot_general(a, b, (((0,), (0,)), ((), ())), preferred_element_type=F32)


def _split3(x):
    hi = x.astype(BF16)
    r = x - hi.astype(F32)
    mid = r.astype(BF16)
    lo = (r - mid.astype(F32)).astype(BF16)
    return hi, mid, lo


def _cumsum_rows(tri_lower, x):
    hi, mid, lo = _split3(x)
    return _dot(tri_lower, hi) + _dot(tri_lower, mid) + _dot(tri_lower, lo)


def _cumsum_lanes(x, tri_upper):
    hi, mid, lo = _split3(x)
    return _dot(hi, tri_upper) + _dot(mid, tri_upper) + _dot(lo, tri_upper)


def _tri(n, lower):
    r = lax.broadcasted_iota(jnp.int32, (n, n), 0)
    c = lax.broadcasted_iota(jnp.int32, (n, n), 1)
    return jnp.where((c <= r) if lower else (r <= c), 1.0, 0.0).astype(BF16)


def _log_sigmoid(z):
    return jnp.minimum(z, 0.0) - jnp.log1p(jnp.exp(-jnp.abs(z)))


def _rms(x):
    return x * lax.rsqrt(jnp.mean(x * x, axis=-1, keepdims=True) + EPS)


def _ada_kernel(ct_ref, w_ref, b_ref, o_ref):
    d, bsz = ct_ref.shape
    tn = w_ref.shape[2]
    rows_per_step = 512

    def body(r, accs):
        rows = pl.ds(pl.multiple_of(r * rows_per_step, rows_per_step), rows_per_step)
        ct = ct_ref[rows, :]
        cond = ct * jax.nn.sigmoid(ct)
        w = w_ref[0, rows, :]
        return tuple(accs[b] + jnp.sum(cond[:, b:b + 1] * w, axis=0, keepdims=True) for b in range(bsz))

    accs = lax.fori_loop(0, d // rows_per_step, body, tuple(jnp.zeros((1, tn), F32) for _ in range(bsz)))
    for b in range(bsz):
        o_ref[0, b:b + 1, :] = accs[b] + b_ref[0]


def _ada_mod(c, ada_w, ada_b):
    bsz, d = c.shape
    depth, _, n = ada_w.shape
    tn = 1024
    return pl.pallas_call(
        _ada_kernel,
        grid=(depth, n // tn),
        in_specs=[pl.BlockSpec((d, bsz), lambda l, j: (0, 0)),
                  pl.BlockSpec((1, d, tn), lambda l, j: (l, 0, j)),
                  pl.BlockSpec((1, 1, tn), lambda l, j: (l, 0, j))],
        out_specs=pl.BlockSpec((1, bsz, tn), lambda l, j: (l, 0, j)),
        out_shape=jax.ShapeDtypeStruct((depth, bsz, n), F32),
        compiler_params=_cparams(("arbitrary", "arbitrary"), 40),
        name="ada_mod",
    )(c.T, ada_w, ada_b.reshape(depth, 1, n))


def _prenorm_kernel(x_ref, w_ref, sc_ref, sh_ref, h_ref):
    y = _rms(x_ref[...]) * w_ref[...]
    h_ref[...] = (y * (1.0 + sc_ref[0]) + sh_ref[0]).astype(BF16)


def _prenorm(x2, w, sc, sh, seq, tm=512):
    n, d = x2.shape
    per_b = seq // tm
    vec = pl.BlockSpec((1, 1, d), lambda i: (i // per_b, 0, 0))
    return pl.pallas_call(
        _prenorm_kernel,
        grid=(n // tm,),
        in_specs=[pl.BlockSpec((tm, d), lambda i: (i, 0)),
                  pl.BlockSpec((1, d), lambda i: (0, 0)), vec, vec],
        out_specs=pl.BlockSpec((tm, d), lambda i: (i, 0)),
        out_shape=jax.ShapeDtypeStruct((n, d), BF16),
        compiler_params=_cparams(("arbitrary",), 32),
        name="prenorm",
    )(x2, w.reshape(1, d), sc, sh)


def _inproj_kernel(h_ref, w_ref, wg_ref, wgt_ref, o_ref, gcol_ref, *grow_ref):
    h = h_ref[...]
    o_ref[...] = _dot(h, w_ref[...]).astype(BF16)

    @pl.when(pl.program_id(1) == 0)
    def _():
        gcol_ref[...] = _dot(h, wg_ref[...])
        if grow_ref:
            grow_ref[0][...] = _dot_nt(wgt_ref[...], h)


def _inproj(h, w, wg, wgt, want_rows, tm=1024, tn=512):
    n, d = h.shape
    ncol = w.shape[1]
    out_shape = [jax.ShapeDtypeStruct((n, ncol), BF16), jax.ShapeDtypeStruct((n, GATE_W), F32)]
    out_specs = [pl.BlockSpec((tm, tn), lambda i, j: (i, j)),
                 pl.BlockSpec((tm, GATE_W), lambda i, j: (i, 0))]
    if want_rows:
        out_shape.append(jax.ShapeDtypeStruct((N_GATE_ROWS, n), F32))
        out_specs.append(pl.BlockSpec((N_GATE_ROWS, tm), lambda i, j: (0, i)))
    return pl.pallas_call(
        _inproj_kernel,
        grid=(n // tm, ncol // tn),
        in_specs=[pl.BlockSpec((tm, d), lambda i, j: (i, 0)),
                  pl.BlockSpec((d, tn), lambda i, j: (0, j)),
                  pl.BlockSpec((d, GATE_W), lambda i, j: (0, 0)),
                  pl.BlockSpec((N_GATE_ROWS, d), lambda i, j: (0, 0))],
        out_specs=out_specs,
        out_shape=out_shape,
        compiler_params=_cparams(("arbitrary", "arbitrary"), 40),
        name="inproj",
    )(h, w, wg, wgt)


def _mlstm_kernel(q_ref, k_ref, v_ref, og_ref, gcol_ref, grow_ref, bcol_ref, brow_ref, nw_ref,
                  y_ref, fcol_ref, frow_ref, c_sc, n_sc, m_sc, carry_col, carry_row):
    L = M_CHUNK

    @pl.when(pl.program_id(1) == 0)
    def _():
        c_sc[...] = jnp.zeros_like(c_sc)
        n_sc[...] = jnp.zeros_like(n_sc)
        m_sc[...] = jnp.zeros_like(m_sc)
        carry_col[...] = jnp.zeros_like(carry_col)
        carry_row[...] = jnp.zeros_like(carry_row)

    gcol = gcol_ref[0] + bcol_ref[...]
    grow = grow_ref[...] + brow_ref[:, 0:1]
    tri_l = _tri(L, True)
    tri_u = _tri(L, False)
    cum_col = _cumsum_rows(tri_l, _log_sigmoid(gcol))
    cum_row = _cumsum_lanes(_log_sigmoid(grow), tri_u)

    fcol = cum_col + carry_col[0:1, :]
    frow = cum_row + carry_row[:, 0:1]
    fcol_ref[0] = fcol
    frow_ref[0] = frow[M_HEADS * 2:, :]
    carry_col[...] = jnp.broadcast_to(fcol[L - 1:L, :], carry_col.shape)
    carry_row[...] = jnp.broadcast_to(frow[:, L - 1:L], carry_row.shape)

    causal = lax.broadcasted_iota(jnp.int32, (L, L), 0) >= lax.broadcasted_iota(jnp.int32, (L, L), 1)
    for hd in range(M_HEADS):
        q = q_ref[0, :, hd * M_DQK:(hd + 1) * M_DQK]
        k = k_ref[0, :, hd * M_DQK:(hd + 1) * M_DQK]
        v = v_ref[0, :, hd * M_DV:(hd + 1) * M_DV]
        ig_c = gcol[:, hd:hd + 1]
        ig_r = grow[hd:hd + 1, :]
        cum_c = cum_col[:, M_HEADS + hd:M_HEADS + hd + 1]
        cum_r = cum_row[M_HEADS + hd:M_HEADS + hd + 1, :]
        m_prev = m_sc[hd, 0:1, 0:1]
        c_st = c_sc[hd]
        n_st = n_sc[hd, 0:1, :]

        dmat = jnp.where(causal, cum_c - cum_r + ig_r, -jnp.inf)
        inter = cum_c + m_prev
        m_t = jnp.maximum(inter, jnp.max(dmat, axis=-1, keepdims=True))
        s = _dot_nt(q, k) * jnp.exp(dmat - m_t)
        g = jnp.exp(inter - m_t)
        qf = q.astype(F32)
        num = _dot(s.astype(BF16), v) + g * _dot(q, c_st.astype(BF16))
        den = jnp.sum(s, axis=-1, keepdims=True) + g * jnp.sum(qf * n_st, axis=-1, keepdims=True)
        h_out = num / jnp.maximum(jnp.abs(den), jnp.exp(-m_t))

        last = cum_c[L - 1:L, :]
        dec = last - cum_c + ig_c
        m_new = jnp.maximum(last + m_prev, jnp.max(dec, axis=0, keepdims=True))
        ws = jnp.exp(dec - m_new)
        gs = jnp.exp(last + m_prev - m_new)
        kw = k.astype(F32) * ws
        c_sc[hd] = gs * c_st + _dot_tn(kw.astype(BF16), v)
        n_sc[hd] = jnp.broadcast_to(gs * n_st + jnp.sum(kw, axis=0, keepdims=True), n_sc.shape[1:])
        m_sc[hd] = jnp.broadcast_to(m_new, m_sc.shape[1:])

        og = og_ref[0, :, hd * M_DV:(hd + 1) * M_DV].astype(F32)
        hn = _rms(h_out) * nw_ref[:, hd * M_DV:(hd + 1) * M_DV]
        y_ref[0, :, hd * M_DV:(hd + 1) * M_DV] = (hn * jax.nn.sigmoid(og)).astype(BF16)


def _mlstm(u, gcol, grow, bias_col, bias_row, norm_w):
    bsz, seq, _ = u.shape
    L = M_CHUNK
    nc = seq // L
    qk_w = M_HEADS * M_DQK
    v_w = M_HEADS * M_DV
    return pl.pallas_call(
        _mlstm_kernel,
        grid=(bsz, nc),
        in_specs=[pl.BlockSpec((1, L, qk_w), lambda b, c: (b, c, 0)),
                  pl.BlockSpec((1, L, qk_w), lambda b, c: (b, c, 1)),
                  pl.BlockSpec((1, L, v_w), lambda b, c: (b, c, 1)),
                  pl.BlockSpec((1, L, v_w), lambda b, c: (b, c, 2)),
                  pl.BlockSpec((1, L, GATE_W), lambda b, c: (b, c, 0)),
                  pl.BlockSpec((N_GATE_ROWS, L), lambda b, c: (0, b * nc + c)),
                  pl.BlockSpec((1, GATE_W), lambda b, c: (0, 0)),
                  pl.BlockSpec((N_GATE_ROWS, LANE), lambda b, c: (0, 0)),
                  pl.BlockSpec((1, v_w), lambda b, c: (0, 0))],
        out_specs=[pl.BlockSpec((1, L, v_w), lambda b, c: (b, c, 0)),
                   pl.BlockSpec((1, L, GATE_W), lambda b, c: (b, c, 0)),
                   pl.BlockSpec((1, F_HEADS, L), lambda b, c: (b, 0, c))],
        out_shape=[jax.ShapeDtypeStruct((bsz, seq, v_w), BF16),
                   jax.ShapeDtypeStruct((bsz, seq, GATE_W), F32),
                   jax.ShapeDtypeStruct((bsz, F_HEADS, seq), F32)],
        scratch_shapes=[pltpu.VMEM((M_HEADS, M_DQK, M_DV), F32),
                        pltpu.VMEM((M_HEADS, SUBLANE, M_DQK), F32),
                        pltpu.VMEM((M_HEADS, SUBLANE, LANE), F32),
                        pltpu.VMEM((SUBLANE, GATE_W), F32),
                        pltpu.VMEM((N_GATE_ROWS, LANE), F32)],
        compiler_params=_cparams(("arbitrary", "arbitrary"), 40),
        name="mlstm",
    )(u, u, u, u, gcol, grow, bias_col, bias_row, norm_w)


def _fox_kernel(q_ref, k_ref, v_ref, fcol_ref, frow_ref, o_ref, k_sc, vt_sc, ck_sc, m_sc, l_sc, acc_sc,
                za_sc, zb_sc):
    tq, C = FOX_T, FOX_COLS
    hd = pl.program_id(1)
    qi = pl.program_id(2)
    q = q_ref[0]
    here = pl.ds(pl.multiple_of(qi * tq, tq), tq)

    k_sc[here, :] = k_ref[0]
    vt_sc[:, here] = v_ref[0].astype(F32).T.astype(BF16)
    lane = lax.broadcasted_iota(jnp.int32, (tq, GATE_W), 1)
    ck = jnp.sum(jnp.where(lane == 2 * M_HEADS + hd, fcol_ref[0], 0.0), axis=-1, keepdims=True)
    ck_sc[here, :] = jnp.broadcast_to(ck * LOG2E, (tq, LANE))
    cq = frow_ref[0, pl.ds(hd, 1), here] * LOG2E

    m_sc[...] = jnp.full_like(m_sc, -jnp.inf)
    l_sc[...] = jnp.zeros_like(l_sc)
    acc_sc[...] = jnp.zeros_like(acc_sc)

    groups = tq // C

    def scores(j, g):
        off = pl.multiple_of(j * tq, tq)
        ckr = ck_sc[pl.ds(off, tq), :]
        return (_dot_nt(k_sc[pl.ds(off, tq), :], q[g * C:(g + 1) * C])
                - jnp.concatenate([ckr] * (C // LANE), axis=1))

    def update(j, g, z):
        off = pl.multiple_of(j * tq, tq)
        cs = slice(g * C, (g + 1) * C)
        nk = z.shape[0]
        m_old = m_sc[:, cs]
        m_new = jnp.maximum(m_old, jnp.max(z, axis=0, keepdims=True) + cq[:, cs])
        p = jnp.exp2(z - (m_new - cq[:, cs]))
        alpha = jnp.exp2(m_old - m_new)
        l_sc[:, cs] = alpha * l_sc[:, cs] + jnp.sum(p, axis=0, keepdims=True)
        acc_sc[:, cs] = alpha * acc_sc[:, cs] + _dot(vt_sc[:, pl.ds(off, nk)], p.astype(BF16))
        m_sc[:, cs] = m_new

    def step(j, z_cur, z_nxt):
        for g in range(groups):
            cs = slice(g * C, (g + 1) * C)
            z_nxt[:, cs] = scores(j + 1, g)
            update(j, g, z_cur[:, cs])

    def diagonal(z_ref):
        for g in range(groups):
            nk = (g + 1) * C
            r = lax.broadcasted_iota(jnp.int32, (nk, C), 0)
            c = lax.broadcasted_iota(jnp.int32, (nk, C), 1) + g * C
            update(qi, g, jnp.where(r <= c, z_ref[0:nk, g * C:(g + 1) * C], -jnp.inf))

    for g in range(groups):
        za_sc[:, g * C:(g + 1) * C] = scores(0, g)

    def pair(i, carry):
        step(2 * i, za_sc, zb_sc)
        step(2 * i + 1, zb_sc, za_sc)
        return carry

    lax.fori_loop(0, qi // 2, pair, 0)
    odd = qi % 2 == 1

    @pl.when(odd)
    def _():
        step(qi - 1, za_sc, zb_sc)
        diagonal(zb_sc)

    @pl.when(jnp.logical_not(odd))
    def _():
        diagonal(za_sc)

    o_ref[0] = (acc_sc[...] / l_sc[...]).T.astype(BF16)


def _fox(u, fcol, frow):
    bsz, seq, _ = u.shape
    tq = FOX_T
    q_blk = (M_HEADS * (2 * M_DQK + 2 * M_DV)) // F_DH
    k_blk = q_blk + F_HEADS
    v_blk = k_blk + F_HEADS
    return pl.pallas_call(
        _fox_kernel,
        grid=(bsz, F_HEADS, seq // tq),
        in_specs=[pl.BlockSpec((1, tq, F_DH), lambda b, h, i: (b, i, q_blk + h)),
                  pl.BlockSpec((1, tq, F_DH), lambda b, h, i: (b, i, k_blk + h)),
                  pl.BlockSpec((1, tq, F_DH), lambda b, h, i: (b, i, v_blk + h)),
                  pl.BlockSpec((1, tq, GATE_W), lambda b, h, i: (b, i, 0)),
                  pl.BlockSpec((1, F_HEADS, seq), lambda b, h, i: (b, 0, 0))],
        out_specs=pl.BlockSpec((1, tq, F_DH), lambda b, h, i: (b, i, h)),
        out_shape=jax.ShapeDtypeStruct((bsz, seq, F_HEADS * F_DH), BF16),
        scratch_shapes=[pltpu.VMEM((seq, F_DH), BF16), pltpu.VMEM((F_DH, seq), BF16),
                        pltpu.VMEM((seq, LANE), F32),
                        pltpu.VMEM((1, tq), F32), pltpu.VMEM((1, tq), F32), pltpu.VMEM((F_DH, tq), F32),
                        pltpu.VMEM((tq, tq), F32), pltpu.VMEM((tq, tq), F32)],
        compiler_params=_cparams(("arbitrary", "arbitrary", "arbitrary"), 48),
        name="fox",
    )(u, u, u, fcol, frow)


def _gla_kernel(q_ref, k_ref, v_ref, og_ref, ga_ref, aw_ref, ab_ref, nw_ref, o_ref, st_sc):
    L, S = G_CHUNK, G_SUB
    nsub = L // S

    @pl.when(pl.program_id(2) == 0)
    def _():
        st_sc[...] = jnp.zeros_like(st_sc)

    tri_l = _tri(L, True)
    row = lax.broadcasted_iota(jnp.int32, (L, 1), 0)
    lane_l = lax.broadcasted_iota(jnp.int32, (S, L), 1)
    row_s = lax.broadcasted_iota(jnp.int32, (S, L), 0)

    def head_chunk(hh, rows):
        dk, dv = slice(hh * G_DK, (hh + 1) * G_DK), slice(hh * G_DV, (hh + 1) * G_DV)
        q = q_ref[0, rows, dk].astype(F32) * (G_DK ** -0.5)
        k = k_ref[0, rows, dk].astype(F32)
        v = v_ref[0, rows, dv]
        z = _dot(ga_ref[0, rows, :].astype(BF16), aw_ref[:, dk]) + ab_ref[:, dk]
        la = _log_sigmoid(z) * (LOG2E / G_TAU)
        c = _cumsum_rows(tri_l, la)
        c_last = c[L - 1:L, :]
        st = st_sc[hh]

        o = _dot_nt((q * jnp.exp2(c)).astype(BF16), st.astype(BF16))

        ref_c = c
        for i in range(1, nsub):
            ref_c = jnp.where(row >= i * S, c[i * S - 1:i * S, :], ref_c)
        qt = (q * jnp.exp2(c - ref_c)).astype(BF16)
        a_rows = [jnp.zeros((S, L), F32)]
        for i in range(1, nsub):
            e = jnp.where(row < i * S, c[i * S - 1:i * S, :] - c, -jnp.inf)
            kt = (k * jnp.exp2(e)).astype(BF16)
            a_rows.append(_dot_nt(qt[i * S:(i + 1) * S, :], kt))

        for i in range(nsub):
            qi = q[i * S:(i + 1) * S, :]
            ci = c[i * S:(i + 1) * S, :]
            a_d = a_rows[i]
            for sl in range(S):
                s_abs = i * S + sl
                e = jnp.minimum(ci - c[s_abs:s_abs + 1, :], 0.0)
                col = jnp.sum(qi * (k[s_abs:s_abs + 1, :] * jnp.exp2(e)), axis=-1, keepdims=True)
                a_d = jnp.where((lane_l == s_abs) & (row_s >= sl), col, a_d)
            a_rows[i] = a_d
        a = jnp.concatenate(a_rows, axis=0)
        o = o + _dot(a.astype(BF16), v)

        kd = (k * jnp.exp2(c_last - c)).astype(BF16)
        st_sc[hh] = st * jnp.exp2(c_last) + _dot_tn(v, kd)

        og = og_ref[0, rows, dv].astype(F32)
        on = _rms(o) * nw_ref[:, dv]
        o_ref[0, rows, dv] = (on * (og * jax.nn.sigmoid(og))).astype(BF16)

    def chunk(cc, carry):
        rows = pl.ds(pl.multiple_of(cc * L, L), L)
        for hh in range(G_HPB):
            head_chunk(hh, rows)
        return carry

    lax.fori_loop(0, G_BLOCK // L, chunk, 0)


def _gla(u, gcol, aw2p, ab, norm_w):
    bsz, seq, _ = u.shape
    blk = G_BLOCK
    wk, wv = G_HPB * G_DK, G_HPB * G_DV
    kq = (G_HEADS * G_DK) // wk
    kv = (2 * G_HEADS * G_DK) // wv
    ko = kv + G_HEADS // G_HPB
    return pl.pallas_call(
        _gla_kernel,
        grid=(bsz, G_HEADS // G_HPB, seq // blk),
        in_specs=[pl.BlockSpec((1, blk, wk), lambda b, h, c: (b, c, h)),
                  pl.BlockSpec((1, blk, wk), lambda b, h, c: (b, c, kq + h)),
                  pl.BlockSpec((1, blk, wv), lambda b, h, c: (b, c, kv + h)),
                  pl.BlockSpec((1, blk, wv), lambda b, h, c: (b, c, ko + h)),
                  pl.BlockSpec((1, blk, GATE_W), lambda b, h, c: (b, c, 0)),
                  pl.BlockSpec((GATE_W, wk), lambda b, h, c: (0, h)),
                  pl.BlockSpec((1, wk), lambda b, h, c: (0, h)),
                  pl.BlockSpec((1, wv), lambda b, h, c: (0, h))],
        out_specs=pl.BlockSpec((1, blk, wv), lambda b, h, c: (b, c, h)),
        out_shape=jax.ShapeDtypeStruct((bsz, seq, G_HEADS * G_DV), BF16),
        scratch_shapes=[pltpu.VMEM((G_HPB, G_DV, G_DK), F32)],
        compiler_params=_cparams(("arbitrary", "arbitrary", "arbitrary"), 40),
        name="gla",
    )(u, u, u, u, gcol, aw2p, ab, norm_w)


def _outproj_kernel(part_ks, nk, has_next, *refs):
    nparts = len(part_ks)
    a_refs = refs[:nparts]
    w_ref, x_ref, pw_ref, gate_ref = refs[nparts:nparts + 4]
    pos = nparts + 4
    if has_next:
        nw_ref, sc_ref, sh_ref = refs[pos:pos + 3]
        pos += 3
    xo_ref = refs[pos]
    pos += 1
    if has_next:
        ho_ref = refs[pos]
        pos += 1

    def finish(y):
        xn = x_ref[...] + _rms(y) * (gate_ref[0] * pw_ref[...])
        xo_ref[...] = xn
        if has_next:
            ho_ref[...] = (_rms(xn) * (nw_ref[...] * (1.0 + sc_ref[0])) + sh_ref[0]).astype(BF16)

    if nk == 1:
        y, off = None, 0
        for p in range(nparts):
            d = _dot(a_refs[p][...], w_ref[off:off + part_ks[p], :])
            y = d if y is None else y + d
            off += part_ks[p]
        finish(y)
    else:
        acc = refs[pos]
        kk = pl.program_id(1)
        d = _dot(a_refs[0][...], w_ref[...])

        @pl.when(kk == 0)
        def _():
            acc[...] = d

        @pl.when(jnp.logical_and(kk > 0, kk < nk - 1))
        def _():
            acc[...] += d

        @pl.when(kk == nk - 1)
        def _():
            finish(acc[...] + d)


def _outproj(parts, w, x2, post_w, gate, nxt, seq, tm=512, tk=None):
    n, d = x2.shape
    per_b = seq // tm
    part_ks = tuple(p.shape[1] for p in parts)
    ktot = sum(part_ks)
    if tk is None:
        nk = 1
        in_specs = [pl.BlockSpec((tm, kp), lambda i, k: (i, 0)) for kp in part_ks]
        in_specs.append(pl.BlockSpec((ktot, d), lambda i, k: (0, 0)))
    else:
        assert len(parts) == 1 and ktot % tk == 0
        nk = ktot // tk
        in_specs = [pl.BlockSpec((tm, tk), lambda i, k: (i, k)), pl.BlockSpec((tk, d), lambda i, k: (k, 0))]
    has_next = nxt is not None
    vec = pl.BlockSpec((1, 1, d), lambda i, k: (i // per_b, 0, 0))
    row = pl.BlockSpec((1, d), lambda i, k: (0, 0))
    in_specs += [pl.BlockSpec((tm, d), lambda i, k: (i, 0)), row, vec]
    args = list(parts) + [w, x2, post_w.reshape(1, d), gate]
    out_shape = [jax.ShapeDtypeStruct((n, d), F32)]
    out_specs = [pl.BlockSpec((tm, d), lambda i, k: (i, 0))]
    if has_next:
        nw, sc, sh = nxt
        in_specs += [row, vec, vec]
        args += [nw.reshape(1, d), sc, sh]
        out_shape.append(jax.ShapeDtypeStruct((n, d), BF16))
        out_specs.append(pl.BlockSpec((tm, d), lambda i, k: (i, 0)))
    res = pl.pallas_call(
        functools.partial(_outproj_kernel, part_ks, nk, has_next),
        grid=(n // tm, nk),
        in_specs=in_specs,
        out_specs=out_specs,
        out_shape=out_shape,
        scratch_shapes=[pltpu.VMEM((tm, d), F32)] if nk > 1 else [],
        compiler_params=_cparams(("arbitrary", "arbitrary"), 56),
        name="outproj",
    )(*args)
    return (res[0], res[1]) if has_next else (res[0], None)


def _ffn_in_kernel(seq, h_ref, wa_ref, wg_ref, cwa_ref, cwg_ref, cba_ref, cbg_ref, o_ref, ea, eg):
    tm = h_ref.shape[0]
    H = CONV_HALO
    m = pl.program_id(1)

    @pl.when((m * tm) % seq == 0)
    def _():
        ea[0:H, :] = jnp.zeros((H, ea.shape[1]), F32)
        eg[0:H, :] = jnp.zeros((H, eg.shape[1]), F32)

    h = h_ref[...]

    def conv(w_ref, cw_ref, cb_ref, ext, scale):
        u = _dot(h, w_ref[...])
        ext[H:H + tm, :] = u
        xe = ext[...]
        u1 = pltpu.roll(xe, 1, 0)[H:, :]
        u2 = pltpu.roll(xe, 2, 0)[H:, :]
        y = ((scale * cw_ref[0:1, :]) * u2 + (scale * cw_ref[1:2, :]) * u1
             + (scale * cw_ref[2:3, :]) * u + scale * cb_ref[...])
        ext[0:H, :] = ext[tm:tm + H, :]
        return y

    a_half = conv(wa_ref, cwa_ref, cba_ref, ea, 0.5)
    g = conv(wg_ref, cwg_ref, cbg_ref, eg, 1.0)
    inner = g * (GELU_C0 + (GELU_C0 * GELU_C1) * (g * g))
    o_ref[...] = ((a_half * g) * (1.0 + jnp.tanh(inner))).astype(BF16)


def _ffn_in(h, w, conv_w, conv_b, seq, tm=1024, tn=512):
    n, d = h.shape
    nj = D_FF // tn
    return pl.pallas_call(
        functools.partial(_ffn_in_kernel, seq),
        grid=(nj, n // tm),
        in_specs=[pl.BlockSpec((tm, d), lambda j, m: (m, 0)),
                  pl.BlockSpec((d, tn), lambda j, m: (0, j)),
                  pl.BlockSpec((d, tn), lambda j, m: (0, nj + j)),
                  pl.BlockSpec((CONV_W, tn), lambda j, m: (0, j)),
                  pl.BlockSpec((CONV_W, tn), lambda j, m: (0, nj + j)),
                  pl.BlockSpec((1, tn), lambda j, m: (0, j)),
                  pl.BlockSpec((1, tn), lambda j, m: (0, nj + j))],
        out_specs=pl.BlockSpec((tm, tn), lambda j, m: (m, j)),
        out_shape=jax.ShapeDtypeStruct((n, D_FF), BF16),
        scratch_shapes=[pltpu.VMEM((tm + CONV_HALO, tn), F32), pltpu.VMEM((tm + CONV_HALO, tn), F32)],
        compiler_params=_cparams(("arbitrary", "arbitrary"), 48),
        name="ffn_in",
    )(h, w, w, conv_w, conv_w, conv_b, conv_b)


def _even_weights(w_in, m_gate_b, fox_f_b):
    d = w_in.shape[0]
    qk, vv, fw = M_HEADS * M_DQK, M_HEADS * M_DV, F_HEADS * F_DH
    o = 0
    mq = w_in[:, o:o + qk]; o += qk
    mk = w_in[:, o:o + qk]; o += qk
    mv = w_in[:, o:o + vv]; o += vv
    mo = w_in[:, o:o + vv]; o += vv
    mg = w_in[:, o:o + 2 * M_HEADS]; o += 2 * M_HEADS
    fq = w_in[:, o:o + fw]; o += fw
    fk = w_in[:, o:o + fw]; o += fw
    fv = w_in[:, o:o + fw]; o += fw
    ff = w_in[:, o:o + F_HEADS]
    main = jnp.concatenate([mq * (M_DQK ** -0.5), mk, mv, mo, fq * (F_DH ** -0.5 * LOG2E), fk, fv], axis=1).astype(BF16)
    gates = jnp.concatenate([mg, ff], axis=1)
    wg = jnp.pad(gates, ((0, 0), (0, GATE_W - N_GATE_ROWS))).astype(BF16)
    wgt = gates.T.astype(BF16)
    bias = jnp.concatenate([m_gate_b, fox_f_b])
    bias_col = jnp.pad(bias, (0, GATE_W - N_GATE_ROWS)).reshape(1, GATE_W)
    bias_row = jnp.broadcast_to(bias[:, None], (N_GATE_ROWS, LANE))
    return main, wg, wgt, bias_col, bias_row


def _odd_weights(w_in, a_w2):
    main = w_in[:, :MIX_W].astype(BF16)
    ga = w_in[:, MIX_W:MIX_W + G_RANK]
    wg = jnp.pad(ga, ((0, 0), (0, GATE_W - G_RANK))).astype(BF16)
    wgt = jnp.zeros((N_GATE_ROWS, w_in.shape[0]), BF16)
    aw2p = jnp.pad(a_w2, ((0, GATE_W - G_RANK), (0, 0))).astype(BF16)
    return main, wg, wgt, aw2p


def kernel(x, c, ada_w, ada_b, norm_w, even_in_w, m_gate_b, m_norm_w, fox_f_b, even_out_w, odd_in_w,
           gla_a_w2, gla_a_b, gla_norm_w, odd_out_w, ffn_in_w, ffn_conv_w, ffn_conv_b, ffn_out_w):
    bsz, seq, d = x.shape
    n = bsz * seq
    mod = _ada_mod(c, ada_w, ada_b)
    mod = mod.reshape(DEPTH, bsz, 6, 1, d)

    def mvec(layer, idx):
        return mod[layer, :, idx]

    x2 = x.reshape(n, d)
    h = _prenorm(x2, norm_w[0, 0], mvec(0, 1), mvec(0, 0), seq)
    for layer in range(DEPTH):
        j = layer // 2
        if layer % 2 == 0:
            main, wg, wgt, bias_col, bias_row = _even_weights(even_in_w[j], m_gate_b[j], fox_f_b[j])
            u, gcol, grow = _inproj(h, main, wg, wgt, True)
            u3 = u.reshape(bsz, seq, MIX_W)
            ym, fcol, frow = _mlstm(u3, gcol.reshape(bsz, seq, GATE_W), grow, bias_col, bias_row,
                                    m_norm_w[j].reshape(1, -1))
            yf = _fox(u3, fcol, frow)
            parts = [ym.reshape(n, -1), yf.reshape(n, -1)]
            w_out = even_out_w[j].astype(BF16)
        else:
            main, wg, wgt, aw2p = _odd_weights(odd_in_w[j], gla_a_w2[j])
            u, gcol = _inproj(h, main, wg, wgt, False)
            yo = _gla(u.reshape(bsz, seq, MIX_W), gcol.reshape(bsz, seq, GATE_W), aw2p,
                      gla_a_b[j].reshape(1, -1), gla_norm_w[j].reshape(1, -1))
            parts = [yo.reshape(n, -1)]
            w_out = odd_out_w[j].astype(BF16)
        x2, h = _outproj(parts, w_out, x2, norm_w[layer, 1], mvec(layer, 2),
                         (norm_w[layer, 2], mvec(layer, 4), mvec(layer, 3)), seq)
        a = _ffn_in(h, ffn_in_w[layer].astype(BF16), ffn_conv_w[layer].reshape(CONV_W, 2 * D_FF),
                    ffn_conv_b[layer].reshape(1, 2 * D_FF), seq)
        nxt = None
        if layer + 1 < DEPTH:
            nxt = (norm_w[layer + 1, 0], mvec(layer + 1, 1), mvec(layer + 1, 0))
        x2, h = _outproj([a], ffn_out_w[layer].astype(BF16), x2, norm_w[layer, 3], mvec(layer, 5), nxt, seq,
                         tk=FFN_OUT_TK)
    return x2.reshape(bsz, seq, d)
```

```python
import functools

import jax
import jax.numpy as jnp
from jax import lax
from jax.experimental import pallas as pl
from jax.experimental.pallas import tpu as pltpu

F32 = jnp.float32
BF16 = jnp.bfloat16

LANE = 128
SUBLANE = 8
VMEM_BYTES_V7X = 64 * 1024 * 1024

D_MODEL = 2048
DEPTH = 4
EPS = 1e-6
M_HEADS, M_DQK, M_DV = 4, 128, 256
F_HEADS, F_DH = 8, 128
G_HEADS, G_DK, G_DV = 4, 256, 512
G_RANK = 16
G_TAU = 16.0
D_FF = 5632
CONV_W = 3

MIX_W = 6144
GATE_W = LANE
N_GATE_ROWS = 16
M_CHUNK = 256
G_BLOCK = 256
G_CHUNK = 64
G_SUB = 16
G_HPB = 4
FOX_T = 512
FOX_COLS = 512
FOX_ONES = 16
LOG2E = 1.4426950408889634
CONV_HALO = SUBLANE
GELU_C0 = 0.7978845608028654
GELU_C1 = 0.044715
FFN_OUT_TM = 256


def _cparams(sem, vmem_mb):
    return pltpu.CompilerParams(dimension_semantics=sem, vmem_limit_bytes=vmem_mb * 1024 * 1024)


def _dot(a, b):
    return jnp.dot(a, b, preferred_element_type=F32)


def _dot_nt(a, b):
    return lax.dot_general(a, b, (((1,), (1,)), ((), ())), preferred_element_type=F32)


def _dot_tn(a, b):
    return lax.dot_general(a, b, (((0,), (0,)), ((), ())), preferred_element_type=F32)


def _split3(x):
    hi = x.astype(BF16)
    r = x - hi.astype(F32)
    mid = r.astype(BF16)
    lo = (r - mid.astype(F32)).astype(BF16)
    return hi, mid, lo


def _cumsum_rows(tri_lower, x):
    hi, mid, lo = _split3(x)
    return _dot(tri_lower, hi) + _dot(tri_lower, mid) + _dot(tri_lower, lo)


def _cumsum_lanes(x, tri_upper):
    hi, mid, lo = _split3(x)
    return _dot(hi, tri_upper) + _dot(mid, tri_upper) + _dot(lo, tri_upper)


def _tri(n, lower):
    r = lax.broadcasted_iota(jnp.int32, (n, n), 0)
    c = lax.broadcasted_iota(jnp.int32, (n, n), 1)
    return jnp.where((c <= r) if lower else (r <= c), 1.0, 0.0).astype(BF16)


def _log_sigmoid(z):
    return jnp.minimum(z, 0.0) - jnp.log1p(jnp.exp(-jnp.abs(z)))


def _rms(x):
    return x * lax.rsqrt(jnp.mean(x * x, axis=-1, keepdims=True) + EPS)


def _ada_kernel(ct_ref, w_ref, b_ref, o_ref):
    d, bsz = ct_ref.shape
    tn = w_ref.shape[2]
    rows_per_step = 512

    def body(r, accs):
        rows = pl.ds(pl.multiple_of(r * rows_per_step, rows_per_step), rows_per_step)
        ct = ct_ref[rows, :]
        cond = ct * jax.nn.sigmoid(ct)
        w = w_ref[0, rows, :]
        return tuple(accs[b] + jnp.sum(cond[:, b:b + 1] * w, axis=0, keepdims=True) for b in range(bsz))

    accs = lax.fori_loop(0, d // rows_per_step, body, tuple(jnp.zeros((1, tn), F32) for _ in range(bsz)))
    for b in range(bsz):
        o_ref[0, b:b + 1, :] = accs[b] + b_ref[0]


def _ada_mod(c, ada_w, ada_b):
    bsz, d = c.shape
    depth, _, n = ada_w.shape
    tn = 1024
    return pl.pallas_call(
        _ada_kernel,
        grid=(depth, n // tn),
        in_specs=[pl.BlockSpec((d, bsz), lambda l, j: (0, 0)),
                  pl.BlockSpec((1, d, tn), lambda l, j: (l, 0, j)),
                  pl.BlockSpec((1, 1, tn), lambda l, j: (l, 0, j))],
        out_specs=pl.BlockSpec((1, bsz, tn), lambda l, j: (l, 0, j)),
        out_shape=jax.ShapeDtypeStruct((depth, bsz, n), F32),
        compiler_params=_cparams(("arbitrary", "arbitrary"), 40),
        name="ada_mod",
    )(c.T, ada_w, ada_b.reshape(depth, 1, n))


def _prenorm_kernel(x_ref, w_ref, sc_ref, sh_ref, h_ref):
    y = _rms(x_ref[...]) * w_ref[...]
    h_ref[...] = (y * (1.0 + sc_ref[0]) + sh_ref[0]).astype(BF16)


def _prenorm(x2, w, sc, sh, seq, tm=512):
    n, d = x2.shape
    per_b = seq // tm
    vec = pl.BlockSpec((1, 1, d), lambda i: (i // per_b, 0, 0))
    return pl.pallas_call(
        _prenorm_kernel,
        grid=(n // tm,),
        in_specs=[pl.BlockSpec((tm, d), lambda i: (i, 0)),
                  pl.BlockSpec((1, d), lambda i: (0, 0)), vec, vec],
        out_specs=pl.BlockSpec((tm, d), lambda i: (i, 0)),
        out_shape=jax.ShapeDtypeStruct((n, d), BF16),
        compiler_params=_cparams(("arbitrary",), 32),
        name="prenorm",
    )(x2, w.reshape(1, d), sc, sh)


def _inproj_kernel(h_ref, w_ref, wg_ref, wgt_ref, o_ref, gcol_ref, *grow_ref):
    h = h_ref[...]
    o_ref[...] = _dot(h, w_ref[...]).astype(BF16)

    @pl.when(pl.program_id(1) == 0)
    def _():
        gcol_ref[...] = _dot(h, wg_ref[...])
        if grow_ref:
            grow_ref[0][...] = _dot_nt(wgt_ref[...], h)


def _inproj(h, w, wg, wgt, want_rows, tm=1024, tn=1024):
    n, d = h.shape
    ncol = w.shape[1]
    out_shape = [jax.ShapeDtypeStruct((n, ncol), BF16), jax.ShapeDtypeStruct((n, GATE_W), F32)]
    out_specs = [pl.BlockSpec((tm, tn), lambda i, j: (i, j)),
                 pl.BlockSpec((tm, GATE_W), lambda i, j: (i, 0))]
    if want_rows:
        out_shape.append(jax.ShapeDtypeStruct((N_GATE_ROWS, n), F32))
        out_specs.append(pl.BlockSpec((N_GATE_ROWS, tm), lambda i, j: (0, i)))
    return pl.pallas_call(
        _inproj_kernel,
        grid=(n // tm, ncol // tn),
        in_specs=[pl.BlockSpec((tm, d), lambda i, j: (i, 0)),
                  pl.BlockSpec((d, tn), lambda i, j: (0, j)),
                  pl.BlockSpec((d, GATE_W), lambda i, j: (0, 0)),
                  pl.BlockSpec((N_GATE_ROWS, d), lambda i, j: (0, 0))],
        out_specs=out_specs,
        out_shape=out_shape,
        compiler_params=_cparams(("arbitrary", "arbitrary"), 40),
        name="inproj",
    )(h, w, wg, wgt)


def _mlstm_kernel(q_ref, k_ref, v_ref, og_ref, gcol_ref, grow_ref, bcol_ref, brow_ref, nw_ref,
                  y_ref, fcol_ref, frow_ref, c_sc, n_sc, m_sc, carry_col, carry_row):
    L = M_CHUNK

    @pl.when(pl.program_id(1) == 0)
    def _():
        c_sc[...] = jnp.zeros_like(c_sc)
        n_sc[...] = jnp.zeros_like(n_sc)
        m_sc[...] = jnp.zeros_like(m_sc)
        carry_col[...] = jnp.zeros_like(carry_col)
        carry_row[...] = jnp.zeros_like(carry_row)

    gcol = gcol_ref[0] + bcol_ref[...]
    grow = grow_ref[...] + brow_ref[:, 0:1]
    tri_l = _tri(L, True)
    tri_u = _tri(L, False)
    cum_col = _cumsum_rows(tri_l, _log_sigmoid(gcol))
    cum_row = _cumsum_lanes(_log_sigmoid(grow), tri_u)

    fcol = cum_col + carry_col[0:1, :]
    frow = cum_row + carry_row[:, 0:1]
    fcol_ref[0] = fcol
    frow_ref[0] = frow[M_HEADS * 2:, :]
    carry_col[...] = jnp.broadcast_to(fcol[L - 1:L, :], carry_col.shape)
    carry_row[...] = jnp.broadcast_to(frow[:, L - 1:L], carry_row.shape)

    causal = lax.broadcasted_iota(jnp.int32, (L, L), 0) >= lax.broadcasted_iota(jnp.int32, (L, L), 1)
    for hd in range(M_HEADS):
        q = q_ref[0, :, hd * M_DQK:(hd + 1) * M_DQK]
        k = k_ref[0, :, hd * M_DQK:(hd + 1) * M_DQK]
        v = v_ref[0, :, hd * M_DV:(hd + 1) * M_DV]
        ig_c = gcol[:, hd:hd + 1]
        ig_r = grow[hd:hd + 1, :]
        cum_c = cum_col[:, M_HEADS + hd:M_HEADS + hd + 1]
        cum_r = cum_row[M_HEADS + hd:M_HEADS + hd + 1, :]
        m_prev = m_sc[hd, 0:1, 0:1]
        c_st = c_sc[hd]
        n_st = n_sc[hd, 0:1, :]

        dmat = jnp.where(causal, cum_c - cum_r + ig_r, -jnp.inf)
        inter = cum_c + m_prev
        m_t = jnp.maximum(inter, jnp.max(dmat, axis=-1, keepdims=True))
        s = _dot_nt(q, k) * jnp.exp(dmat - m_t)
        g = jnp.exp(inter - m_t)
        qf = q.astype(F32)
        num = _dot(s.astype(BF16), v) + g * _dot(q, c_st.astype(BF16))
        den = jnp.sum(s, axis=-1, keepdims=True) + g * jnp.sum(qf * n_st, axis=-1, keepdims=True)
        h_out = num / jnp.maximum(jnp.abs(den), jnp.exp(-m_t))

        last = cum_c[L - 1:L, :]
        dec = last - cum_c + ig_c
        m_new = jnp.maximum(last + m_prev, jnp.max(dec, axis=0, keepdims=True))
        ws = jnp.exp(dec - m_new)
        gs = jnp.exp(last + m_prev - m_new)
        kw = k.astype(F32) * ws
        c_sc[hd] = gs * c_st + _dot_tn(kw.astype(BF16), v)
        n_sc[hd] = jnp.broadcast_to(gs * n_st + jnp.sum(kw, axis=0, keepdims=True), n_sc.shape[1:])
        m_sc[hd] = jnp.broadcast_to(m_new, m_sc.shape[1:])

        og = og_ref[0, :, hd * M_DV:(hd + 1) * M_DV].astype(F32)
        hn = _rms(h_out) * nw_ref[:, hd * M_DV:(hd + 1) * M_DV]
        y_ref[0, :, hd * M_DV:(hd + 1) * M_DV] = (hn * jax.nn.sigmoid(og)).astype(BF16)


def _mlstm(u, gcol, grow, bias_col, bias_row, norm_w):
    bsz, seq, _ = u.shape
    L = M_CHUNK
    nc = seq // L
    qk_w = M_HEADS * M_DQK
    v_w = M_HEADS * M_DV
    return pl.pallas_call(
        _mlstm_kernel,
        grid=(bsz, nc),
        in_specs=[pl.BlockSpec((1, L, qk_w), lambda b, c: (b, c, 0)),
                  pl.BlockSpec((1, L, qk_w), lambda b, c: (b, c, 1)),
                  pl.BlockSpec((1, L, v_w), lambda b, c: (b, c, 1)),
                  pl.BlockSpec((1, L, v_w), lambda b, c: (b, c, 2)),
                  pl.BlockSpec((1, L, GATE_W), lambda b, c: (b, c, 0)),
                  pl.BlockSpec((N_GATE_ROWS, L), lambda b, c: (0, b * nc + c)),
                  pl.BlockSpec((1, GATE_W), lambda b, c: (0, 0)),
                  pl.BlockSpec((N_GATE_ROWS, LANE), lambda b, c: (0, 0)),
                  pl.BlockSpec((1, v_w), lambda b, c: (0, 0))],
        out_specs=[pl.BlockSpec((1, L, v_w), lambda b, c: (b, c, 0)),
                   pl.BlockSpec((1, L, GATE_W), lambda b, c: (b, c, 0)),
                   pl.BlockSpec((1, F_HEADS, L), lambda b, c: (b, 0, c))],
        out_shape=[jax.ShapeDtypeStruct((bsz, seq, v_w), BF16),
                   jax.ShapeDtypeStruct((bsz, seq, GATE_W), F32),
                   jax.ShapeDtypeStruct((bsz, F_HEADS, seq), F32)],
        scratch_shapes=[pltpu.VMEM((M_HEADS, M_DQK, M_DV), F32),
                        pltpu.VMEM((M_HEADS, SUBLANE, M_DQK), F32),
                        pltpu.VMEM((M_HEADS, SUBLANE, LANE), F32),
                        pltpu.VMEM((SUBLANE, GATE_W), F32),
                        pltpu.VMEM((N_GATE_ROWS, LANE), F32)],
        compiler_params=_cparams(("arbitrary", "arbitrary"), 40),
        name="mlstm",
    )(u, u, u, u, gcol, grow, bias_col, bias_row, norm_w)


def _fox_kernel(q_ref, k_ref, v_ref, fcol_ref, frow_ref, o_ref, k_sc, vt_sc, ck_sc, m_sc, acc_sc, za_sc, zb_sc):
    tq, C = FOX_T, FOX_COLS
    hd = pl.program_id(1)
    qi = pl.program_id(2)
    q = q_ref[0]
    here = pl.ds(pl.multiple_of(qi * tq, tq), tq)

    k_sc[here, :] = k_ref[0]
    vt_sc[0:F_DH, here] = v_ref[0].astype(F32).T.astype(BF16)
    vt_sc[F_DH:, here] = jnp.ones((FOX_ONES, tq), BF16)
    lane = lax.broadcasted_iota(jnp.int32, (tq, GATE_W), 1)
    ck = jnp.sum(jnp.where(lane == 2 * M_HEADS + hd, fcol_ref[0], 0.0), axis=-1, keepdims=True)
    ck_sc[here, :] = jnp.broadcast_to(ck * LOG2E, (tq, LANE))
    cq = frow_ref[0, pl.ds(hd, 1), here] * LOG2E

    m_sc[...] = jnp.full_like(m_sc, -jnp.inf)
    acc_sc[...] = jnp.zeros_like(acc_sc)

    groups = tq // C

    def scores(j, g):
        off = pl.multiple_of(j * tq, tq)
        ckr = ck_sc[pl.ds(off, tq), :]
        return (_dot_nt(k_sc[pl.ds(off, tq), :], q[g * C:(g + 1) * C])
                - jnp.concatenate([ckr] * (C // LANE), axis=1))

    def update(j, g, z):
        off = pl.multiple_of(j * tq, tq)
        cs = slice(g * C, (g + 1) * C)
        nk = z.shape[0]
        m_old = m_sc[:, cs]
        m_new = jnp.maximum(m_old, jnp.max(z, axis=0, keepdims=True) + cq[:, cs])
        p = jnp.exp2(z - (m_new - cq[:, cs]))
        alpha = jnp.exp2(m_old - m_new)
        acc_sc[:, cs] = alpha * acc_sc[:, cs] + _dot(vt_sc[:, pl.ds(off, nk)], p.astype(BF16))
        m_sc[:, cs] = m_new

    def step(j, z_cur, z_nxt):
        for g in range(groups):
            cs = slice(g * C, (g + 1) * C)
            z_nxt[:, cs] = scores(j + 1, g)
            update(j, g, z_cur[:, cs])

    def diagonal(z_ref):
        for g in range(groups):
            nk = (g + 1) * C
            r = lax.broadcasted_iota(jnp.int32, (nk, C), 0)
            c = lax.broadcasted_iota(jnp.int32, (nk, C), 1) + g * C
            update(qi, g, jnp.where(r <= c, z_ref[0:nk, g * C:(g + 1) * C], -jnp.inf))

    for g in range(groups):
        za_sc[:, g * C:(g + 1) * C] = scores(0, g)

    def pair(i, carry):
        step(2 * i, za_sc, zb_sc)
        step(2 * i + 1, zb_sc, za_sc)
        return carry

    lax.fori_loop(0, qi // 2, pair, 0)
    odd = qi % 2 == 1

    @pl.when(odd)
    def _():
        step(qi - 1, za_sc, zb_sc)
        diagonal(zb_sc)

    @pl.when(jnp.logical_not(odd))
    def _():
        diagonal(za_sc)

    o_ref[0] = (acc_sc[0:F_DH, :] / acc_sc[F_DH:F_DH + 1, :]).T.astype(BF16)


def _fox(u, fcol, frow):
    bsz, seq, _ = u.shape
    tq = FOX_T
    q_blk = (M_HEADS * (2 * M_DQK + 2 * M_DV)) // F_DH
    k_blk = q_blk + F_HEADS
    v_blk = k_blk + F_HEADS
    return pl.pallas_call(
        _fox_kernel,
        grid=(bsz, F_HEADS, seq // tq),
        in_specs=[pl.BlockSpec((1, tq, F_DH), lambda b, h, i: (b, i, q_blk + h)),
                  pl.BlockSpec((1, tq, F_DH), lambda b, h, i: (b, i, k_blk + h)),
                  pl.BlockSpec((1, tq, F_DH), lambda b, h, i: (b, i, v_blk + h)),
                  pl.BlockSpec((1, tq, GATE_W), lambda b, h, i: (b, i, 0)),
                  pl.BlockSpec((1, F_HEADS, seq), lambda b, h, i: (b, 0, 0))],
        out_specs=pl.BlockSpec((1, tq, F_DH), lambda b, h, i: (b, i, h)),
        out_shape=jax.ShapeDtypeStruct((bsz, seq, F_HEADS * F_DH), BF16),
        scratch_shapes=[pltpu.VMEM((seq, F_DH), BF16), pltpu.VMEM((F_DH + FOX_ONES, seq), BF16),
                        pltpu.VMEM((seq, LANE), F32),
                        pltpu.VMEM((1, tq), F32), pltpu.VMEM((F_DH + FOX_ONES, tq), F32),
                        pltpu.VMEM((tq, tq), F32), pltpu.VMEM((tq, tq), F32)],
        compiler_params=_cparams(("arbitrary", "arbitrary", "arbitrary"), 48),
        name="fox",
    )(u, u, u, fcol, frow)


def _gla_kernel(q_ref, k_ref, v_ref, og_ref, ga_ref, aw_ref, ab_ref, nw_ref, o_ref, st_sc):
    L, S = G_CHUNK, G_SUB
    nsub = L // S

    @pl.when(pl.program_id(2) == 0)
    def _():
        st_sc[...] = jnp.zeros_like(st_sc)

    tri_l = _tri(L, True)
    row = lax.broadcasted_iota(jnp.int32, (L, 1), 0)
    lane_l = lax.broadcasted_iota(jnp.int32, (S, L), 1)
    row_s = lax.broadcasted_iota(jnp.int32, (S, L), 0)

    def head_chunk(hh, rows):
        dk, dv = slice(hh * G_DK, (hh + 1) * G_DK), slice(hh * G_DV, (hh + 1) * G_DV)
        q = q_ref[0, rows, dk].astype(F32) * (G_DK ** -0.5)
        k = k_ref[0, rows, dk].astype(F32)
        v = v_ref[0, rows, dv]
        z = _dot(ga_ref[0, rows, :].astype(BF16), aw_ref[:, dk]) + ab_ref[:, dk]
        la = _log_sigmoid(z) * (LOG2E / G_TAU)
        c = _cumsum_rows(tri_l, la)
        c_last = c[L - 1:L, :]
        st = st_sc[hh]

        o = _dot_nt((q * jnp.exp2(c)).astype(BF16), st.astype(BF16))

        ref_c = c
        for i in range(1, nsub):
            ref_c = jnp.where(row >= i * S, c[i * S - 1:i * S, :], ref_c)
        qt = (q * jnp.exp2(c - ref_c)).astype(BF16)
        a_rows = [jnp.zeros((S, L), F32)]
        for i in range(1, nsub):
            e = jnp.where(row < i * S, c[i * S - 1:i * S, :] - c, -jnp.inf)
            kt = (k * jnp.exp2(e)).astype(BF16)
            a_rows.append(_dot_nt(qt[i * S:(i + 1) * S, :], kt))

        for i in range(nsub):
            qi = q[i * S:(i + 1) * S, :]
            ci = c[i * S:(i + 1) * S, :]
            a_d = a_rows[i]
            for sl in range(S):
                s_abs = i * S + sl
                e = jnp.minimum(ci - c[s_abs:s_abs + 1, :], 0.0)
                col = jnp.sum(qi * (k[s_abs:s_abs + 1, :] * jnp.exp2(e)), axis=-1, keepdims=True)
                a_d = jnp.where((lane_l == s_abs) & (row_s >= sl), col, a_d)
            a_rows[i] = a_d
        a = jnp.concatenate(a_rows, axis=0)
        o = o + _dot(a.astype(BF16), v)

        kd = (k * jnp.exp2(c_last - c)).astype(BF16)
        st_sc[hh] = st * jnp.exp2(c_last) + _dot_tn(v, kd)

        og = og_ref[0, rows, dv].astype(F32)
        on = _rms(o) * nw_ref[:, dv]
        o_ref[0, rows, dv] = (on * (og * jax.nn.sigmoid(og))).astype(BF16)

    def chunk(cc, carry):
        rows = pl.ds(pl.multiple_of(cc * L, L), L)
        for hh in range(G_HPB):
            head_chunk(hh, rows)
        return carry

    lax.fori_loop(0, G_BLOCK // L, chunk, 0)


def _gla(u, gcol, aw2p, ab, norm_w):
    bsz, seq, _ = u.shape
    blk = G_BLOCK
    wk, wv = G_HPB * G_DK, G_HPB * G_DV
    kq = (G_HEADS * G_DK) // wk
    kv = (2 * G_HEADS * G_DK) // wv
    ko = kv + G_HEADS // G_HPB
    return pl.pallas_call(
        _gla_kernel,
        grid=(bsz, G_HEADS // G_HPB, seq // blk),
        in_specs=[pl.BlockSpec((1, blk, wk), lambda b, h, c: (b, c, h)),
                  pl.BlockSpec((1, blk, wk), lambda b, h, c: (b, c, kq + h)),
                  pl.BlockSpec((1, blk, wv), lambda b, h, c: (b, c, kv + h)),
                  pl.BlockSpec((1, blk, wv), lambda b, h, c: (b, c, ko + h)),
                  pl.BlockSpec((1, blk, GATE_W), lambda b, h, c: (b, c, 0)),
                  pl.BlockSpec((GATE_W, wk), lambda b, h, c: (0, h)),
                  pl.BlockSpec((1, wk), lambda b, h, c: (0, h)),
                  pl.BlockSpec((1, wv), lambda b, h, c: (0, h))],
        out_specs=pl.BlockSpec((1, blk, wv), lambda b, h, c: (b, c, h)),
        out_shape=jax.ShapeDtypeStruct((bsz, seq, G_HEADS * G_DV), BF16),
        scratch_shapes=[pltpu.VMEM((G_HPB, G_DV, G_DK), F32)],
        compiler_params=_cparams(("arbitrary", "arbitrary", "arbitrary"), 40),
        name="gla",
    )(u, u, u, u, gcol, aw2p, ab, norm_w)


def _outproj_kernel(part_ks, has_next, *refs):
    nparts = len(part_ks)
    a_refs = refs[:nparts]
    w_ref, x_ref, pw_ref, gate_ref = refs[nparts:nparts + 4]
    pos = nparts + 4
    if has_next:
        nw_ref, sc_ref, sh_ref = refs[pos:pos + 3]
        pos += 3
    xo_ref = refs[pos]
    pos += 1
    if has_next:
        ho_ref = refs[pos]
        pos += 1

    y, off = None, 0
    for p in range(nparts):
        d = _dot(a_refs[p][...], w_ref[off:off + part_ks[p], :])
        y = d if y is None else y + d
        off += part_ks[p]
    xn = x_ref[...] + _rms(y) * (gate_ref[0] * pw_ref[...])
    xo_ref[...] = xn
    if has_next:
        ho_ref[...] = (_rms(xn) * (nw_ref[...] * (1.0 + sc_ref[0])) + sh_ref[0]).astype(BF16)


def _outproj(parts, w, x2, post_w, gate, nxt, seq, tm=512):
    n, d = x2.shape
    per_b = seq // tm
    part_ks = tuple(p.shape[1] for p in parts)
    ktot = sum(part_ks)
    in_specs = [pl.BlockSpec((tm, kp), lambda i, k: (i, 0)) for kp in part_ks]
    in_specs.append(pl.BlockSpec((ktot, d), lambda i, k: (0, 0), pipeline_mode=pl.Buffered(1)))
    has_next = nxt is not None
    vec = pl.BlockSpec((1, 1, d), lambda i, k: (i // per_b, 0, 0))
    row = pl.BlockSpec((1, d), lambda i, k: (0, 0))
    in_specs += [pl.BlockSpec((tm, d), lambda i, k: (i, 0)), row, vec]
    args = list(parts) + [w, x2, post_w.reshape(1, d), gate]
    out_shape = [jax.ShapeDtypeStruct((n, d), F32)]
    out_specs = [pl.BlockSpec((tm, d), lambda i, k: (i, 0))]
    if has_next:
        nw, sc, sh = nxt
        in_specs += [row, vec, vec]
        args += [nw.reshape(1, d), sc, sh]
        out_shape.append(jax.ShapeDtypeStruct((n, d), BF16))
        out_specs.append(pl.BlockSpec((tm, d), lambda i, k: (i, 0)))
    res = pl.pallas_call(
        functools.partial(_outproj_kernel, part_ks, has_next),
        grid=(n // tm, 1),
        in_specs=in_specs,
        out_specs=out_specs,
        out_shape=out_shape,
        compiler_params=_cparams(("arbitrary", "arbitrary"), 56),
        name="outproj",
    )(*args)
    return (res[0], res[1]) if has_next else (res[0], None)


def _ffn_in_kernel(seq, h_ref, wa_ref, wg_ref, cwa_ref, cwg_ref, cba_ref, cbg_ref, o_ref, ea, eg):
    tm = h_ref.shape[0]
    H = CONV_HALO
    m = pl.program_id(1)

    @pl.when((m * tm) % seq == 0)
    def _():
        ea[0:H, :] = jnp.zeros((H, ea.shape[1]), F32)
        eg[0:H, :] = jnp.zeros((H, eg.shape[1]), F32)

    h = h_ref[...]

    def conv(w_ref, cw_ref, cb_ref, ext, scale):
        u = _dot(h, w_ref[...])
        ext[H:H + tm, :] = u
        xe = ext[...]
        u1 = pltpu.roll(xe, 1, 0)[H:, :]
        u2 = pltpu.roll(xe, 2, 0)[H:, :]
        y = ((scale * cw_ref[0:1, :]) * u2 + (scale * cw_ref[1:2, :]) * u1
             + (scale * cw_ref[2:3, :]) * u + scale * cb_ref[...])
        ext[0:H, :] = ext[tm:tm + H, :]
        return y

    a_half = conv(wa_ref, cwa_ref, cba_ref, ea, 0.5)
    g = conv(wg_ref, cwg_ref, cbg_ref, eg, 1.0)
    inner = g * (GELU_C0 + (GELU_C0 * GELU_C1) * (g * g))
    o_ref[...] = ((a_half * g) * (1.0 + jnp.tanh(inner))).astype(BF16)


def _ffn_in(h, w, conv_w, conv_b, seq, tm=1024, tn=512):
    n, d = h.shape
    nj = D_FF // tn
    return pl.pallas_call(
        functools.partial(_ffn_in_kernel, seq),
        grid=(nj, n // tm),
        in_specs=[pl.BlockSpec((tm, d), lambda j, m: (m, 0)),
                  pl.BlockSpec((d, tn), lambda j, m: (0, j)),
                  pl.BlockSpec((d, tn), lambda j, m: (0, nj + j)),
                  pl.BlockSpec((CONV_W, tn), lambda j, m: (0, j)),
                  pl.BlockSpec((CONV_W, tn), lambda j, m: (0, nj + j)),
                  pl.BlockSpec((1, tn), lambda j, m: (0, j)),
                  pl.BlockSpec((1, tn), lambda j, m: (0, nj + j))],
        out_specs=pl.BlockSpec((tm, tn), lambda j, m: (m, j)),
        out_shape=jax.ShapeDtypeStruct((n, D_FF), BF16),
        scratch_shapes=[pltpu.VMEM((tm + CONV_HALO, tn), F32), pltpu.VMEM((tm + CONV_HALO, tn), F32)],
        compiler_params=_cparams(("arbitrary", "arbitrary"), 48),
        name="ffn_in",
    )(h, w, w, conv_w, conv_w, conv_b, conv_b)


def _even_weights(w_in, m_gate_b, fox_f_b):
    d = w_in.shape[0]
    qk, vv, fw = M_HEADS * M_DQK, M_HEADS * M_DV, F_HEADS * F_DH
    o = 0
    mq = w_in[:, o:o + qk]; o += qk
    mk = w_in[:, o:o + qk]; o += qk
    mv = w_in[:, o:o + vv]; o += vv
    mo = w_in[:, o:o + vv]; o += vv
    mg = w_in[:, o:o + 2 * M_HEADS]; o += 2 * M_HEADS
    fq = w_in[:, o:o + fw]; o += fw
    fk = w_in[:, o:o + fw]; o += fw
    fv = w_in[:, o:o + fw]; o += fw
    ff = w_in[:, o:o + F_HEADS]
    main = jnp.concatenate([mq * (M_DQK ** -0.5), mk, mv, mo, fq * (F_DH ** -0.5 * LOG2E), fk, fv], axis=1).astype(BF16)
    gates = jnp.concatenate([mg, ff], axis=1)
    wg = jnp.pad(gates, ((0, 0), (0, GATE_W - N_GATE_ROWS))).astype(BF16)
    wgt = gates.T.astype(BF16)
    bias = jnp.concatenate([m_gate_b, fox_f_b])
    bias_col = jnp.pad(bias, (0, GATE_W - N_GATE_ROWS)).reshape(1, GATE_W)
    bias_row = jnp.broadcast_to(bias[:, None], (N_GATE_ROWS, LANE))
    return main, wg, wgt, bias_col, bias_row


def _odd_weights(w_in, a_w2):
    main = w_in[:, :MIX_W].astype(BF16)
    ga = w_in[:, MIX_W:MIX_W + G_RANK]
    wg = jnp.pad(ga, ((0, 0), (0, GATE_W - G_RANK))).astype(BF16)
    wgt = jnp.zeros((N_GATE_ROWS, w_in.shape[0]), BF16)
    aw2p = jnp.pad(a_w2, ((0, GATE_W - G_RANK), (0, 0))).astype(BF16)
    return main, wg, wgt, aw2p


def kernel(x, c, ada_w, ada_b, norm_w, even_in_w, m_gate_b, m_norm_w, fox_f_b, even_out_w, odd_in_w,
           gla_a_w2, gla_a_b, gla_norm_w, odd_out_w, ffn_in_w, ffn_conv_w, ffn_conv_b, ffn_out_w):
    bsz, seq, d = x.shape
    n = bsz * seq
    mod = _ada_mod(c, ada_w, ada_b)
    mod = mod.reshape(DEPTH, bsz, 6, 1, d)

    def mvec(layer, idx):
        return mod[layer, :, idx]

    x2 = x.reshape(n, d)
    h = _prenorm(x2, norm_w[0, 0], mvec(0, 1), mvec(0, 0), seq)
    for layer in range(DEPTH):
        j = layer // 2
        if layer % 2 == 0:
            main, wg, wgt, bias_col, bias_row = _even_weights(even_in_w[j], m_gate_b[j], fox_f_b[j])
            u, gcol, grow = _inproj(h, main, wg, wgt, True)
            u3 = u.reshape(bsz, seq, MIX_W)
            ym, fcol, frow = _mlstm(u3, gcol.reshape(bsz, seq, GATE_W), grow, bias_col, bias_row,
                                    m_norm_w[j].reshape(1, -1))
            yf = _fox(u3, fcol, frow)
            parts = [ym.reshape(n, -1), yf.reshape(n, -1)]
            w_out = even_out_w[j].astype(BF16)
        else:
            main, wg, wgt, aw2p = _odd_weights(odd_in_w[j], gla_a_w2[j])
            u, gcol = _inproj(h, main, wg, wgt, False)
            yo = _gla(u.reshape(bsz, seq, MIX_W), gcol.reshape(bsz, seq, GATE_W), aw2p,
                      gla_a_b[j].reshape(1, -1), gla_norm_w[j].reshape(1, -1))
            parts = [yo.reshape(n, -1)]
            w_out = odd_out_w[j].astype(BF16)
        x2, h = _outproj(parts, w_out, x2, norm_w[layer, 1], mvec(layer, 2),
                         (norm_w[layer, 2], mvec(layer, 4), mvec(layer, 3)), seq)
        a = _ffn_in(h, ffn_in_w[layer].astype(BF16), ffn_conv_w[layer].reshape(CONV_W, 2 * D_FF),
                    ffn_conv_b[layer].reshape(1, 2 * D_FF), seq)
        nxt = None
        if layer + 1 < DEPTH:
            nxt = (norm_w[layer + 1, 0], mvec(layer + 1, 1), mvec(layer + 1, 0))
        x2, h = _outproj([a], ffn_out_w[layer].astype(BF16), x2, norm_w[layer, 3], mvec(layer, 5), nxt, seq,
                         tm=FFN_OUT_TM)
    return x2.reshape(bsz, seq, d)
```

```python
import functools

import jax
import jax.numpy as jnp
from jax import lax
from jax.experimental import pallas as pl
from jax.experimental.pallas import tpu as pltpu

F32 = jnp.float32
BF16 = jnp.bfloat16

LANE = 128
SUBLANE = 8
VMEM_BYTES_V7X = 64 * 1024 * 1024

D_MODEL = 2048
DEPTH = 4
EPS = 1e-6
M_HEADS, M_DQK, M_DV = 4, 128, 256
F_HEADS, F_DH = 8, 128
G_HEADS, G_DK, G_DV = 4, 256, 512
G_RANK = 16
G_TAU = 16.0
D_FF = 5632
CONV_W = 3

MIX_W = 6144
GATE_W = LANE
N_GATE_ROWS = 16
M_CHUNK = 256
G_BLOCK = 256
G_CHUNK = 64
G_SUB = 16
G_HPB = 4
FOX_T = 512
FOX_COLS = 512
FOX_ONES = 16
LOG2E = 1.4426950408889634
CONV_HALO = SUBLANE
GELU_C0 = 0.7978845608028654
GELU_C1 = 0.044715
FFN_OUT_TM = 256
INPROJ_TM = 256


def _cparams(sem, vmem_mb):
    return pltpu.CompilerParams(dimension_semantics=sem, vmem_limit_bytes=vmem_mb * 1024 * 1024)


def _dot(a, b):
    return jnp.dot(a, b, preferred_element_type=F32)


def _dot_nt(a, b):
    return lax.dot_general(a, b, (((1,), (1,)), ((), ())), preferred_element_type=F32)


def _dot_tn(a, b):
    return lax.dot_general(a, b, (((0,), (0,)), ((), ())), preferred_element_type=F32)


def _split3(x):
    hi = x.astype(BF16)
    r = x - hi.astype(F32)
    mid = r.astype(BF16)
    lo = (r - mid.astype(F32)).astype(BF16)
    return hi, mid, lo


def _cumsum_rows(tri_lower, x):
    hi, mid, lo = _split3(x)
    return _dot(tri_lower, hi) + _dot(tri_lower, mid) + _dot(tri_lower, lo)


def _cumsum_lanes(x, tri_upper):
    hi, mid, lo = _split3(x)
    return _dot(hi, tri_upper) + _dot(mid, tri_upper) + _dot(lo, tri_upper)


def _tri(n, lower):
    r = lax.broadcasted_iota(jnp.int32, (n, n), 0)
    c = lax.broadcasted_iota(jnp.int32, (n, n), 1)
    return jnp.where((c <= r) if lower else (r <= c), 1.0, 0.0).astype(BF16)


def _log_sigmoid(z):
    return jnp.minimum(z, 0.0) - jnp.log1p(jnp.exp(-jnp.abs(z)))


def _rms(x):
    return x * lax.rsqrt(jnp.mean(x * x, axis=-1, keepdims=True) + EPS)


def _ada_kernel(ct_ref, w_ref, b_ref, o_ref):
    d, bsz = ct_ref.shape
    tn = w_ref.shape[2]
    rows_per_step = 512

    def body(r, accs):
        rows = pl.ds(pl.multiple_of(r * rows_per_step, rows_per_step), rows_per_step)
        ct = ct_ref[rows, :]
        cond = ct * jax.nn.sigmoid(ct)
        w = w_ref[0, rows, :]
        return tuple(accs[b] + jnp.sum(cond[:, b:b + 1] * w, axis=0, keepdims=True) for b in range(bsz))

    accs = lax.fori_loop(0, d // rows_per_step, body, tuple(jnp.zeros((1, tn), F32) for _ in range(bsz)))
    for b in range(bsz):
        o_ref[0, b:b + 1, :] = accs[b] + b_ref[0]


def _ada_mod(c, ada_w, ada_b):
    bsz, d = c.shape
    depth, _, n = ada_w.shape
    tn = 1024
    return pl.pallas_call(
        _ada_kernel,
        grid=(depth, n // tn),
        in_specs=[pl.BlockSpec((d, bsz), lambda l, j: (0, 0)),
                  pl.BlockSpec((1, d, tn), lambda l, j: (l, 0, j)),
                  pl.BlockSpec((1, 1, tn), lambda l, j: (l, 0, j))],
        out_specs=pl.BlockSpec((1, bsz, tn), lambda l, j: (l, 0, j)),
        out_shape=jax.ShapeDtypeStruct((depth, bsz, n), F32),
        compiler_params=_cparams(("arbitrary", "arbitrary"), 40),
        name="ada_mod",
    )(c.T, ada_w, ada_b.reshape(depth, 1, n))


def _prenorm_kernel(x_ref, w_ref, sc_ref, sh_ref, h_ref):
    y = _rms(x_ref[...]) * w_ref[...]
    h_ref[...] = (y * (1.0 + sc_ref[0]) + sh_ref[0]).astype(BF16)


def _prenorm(x2, w, sc, sh, seq, tm=512):
    n, d = x2.shape
    per_b = seq // tm
    vec = pl.BlockSpec((1, 1, d), lambda i: (i // per_b, 0, 0))
    return pl.pallas_call(
        _prenorm_kernel,
        grid=(n // tm,),
        in_specs=[pl.BlockSpec((tm, d), lambda i: (i, 0)),
                  pl.BlockSpec((1, d), lambda i: (0, 0)), vec, vec],
        out_specs=pl.BlockSpec((tm, d), lambda i: (i, 0)),
        out_shape=jax.ShapeDtypeStruct((n, d), BF16),
        compiler_params=_cparams(("arbitrary",), 32),
        name="prenorm",
    )(x2, w.reshape(1, d), sc, sh)


def _inproj_kernel(h_ref, w_ref, wg_ref, wgt_ref, o_ref, gcol_ref, *grow_ref):
    h = h_ref[...]
    o_ref[...] = _dot(h, w_ref[...]).astype(BF16)

    @pl.when(pl.program_id(1) == 0)
    def _():
        gcol_ref[...] = _dot(h, wg_ref[...])
        if grow_ref:
            grow_ref[0][...] = _dot_nt(wgt_ref[...], h)


def _inproj(h, w, wg, wgt, want_rows, tm=INPROJ_TM, tn=MIX_W):
    n, d = h.shape
    ncol = w.shape[1]
    out_shape = [jax.ShapeDtypeStruct((n, ncol), BF16), jax.ShapeDtypeStruct((n, GATE_W), F32)]
    out_specs = [pl.BlockSpec((tm, tn), lambda i, j: (i, j)),
                 pl.BlockSpec((tm, GATE_W), lambda i, j: (i, 0))]
    if want_rows:
        out_shape.append(jax.ShapeDtypeStruct((N_GATE_ROWS, n), F32))
        out_specs.append(pl.BlockSpec((N_GATE_ROWS, tm), lambda i, j: (0, i)))
    return pl.pallas_call(
        _inproj_kernel,
        grid=(n // tm, ncol // tn),
        in_specs=[pl.BlockSpec((tm, d), lambda i, j: (i, 0)),
                  pl.BlockSpec((d, tn), lambda i, j: (0, j), pipeline_mode=pl.Buffered(1)),
                  pl.BlockSpec((d, GATE_W), lambda i, j: (0, 0), pipeline_mode=pl.Buffered(1)),
                  pl.BlockSpec((N_GATE_ROWS, d), lambda i, j: (0, 0), pipeline_mode=pl.Buffered(1))],
        out_specs=out_specs,
        out_shape=out_shape,
        compiler_params=_cparams(("arbitrary", "arbitrary"), 40),
        name="inproj",
    )(h, w, wg, wgt)


def _mlstm_kernel(q_ref, k_ref, v_ref, og_ref, gcol_ref, grow_ref, bcol_ref, brow_ref, nw_ref,
                  y_ref, fcol_ref, frow_ref, c_sc, n_sc, m_sc, carry_col, carry_row):
    L = M_CHUNK

    @pl.when(pl.program_id(1) == 0)
    def _():
        c_sc[...] = jnp.zeros_like(c_sc)
        n_sc[...] = jnp.zeros_like(n_sc)
        m_sc[...] = jnp.zeros_like(m_sc)
        carry_col[...] = jnp.zeros_like(carry_col)
        carry_row[...] = jnp.zeros_like(carry_row)

    gcol = gcol_ref[0] + bcol_ref[...]
    grow = grow_ref[...] + brow_ref[:, 0:1]
    tri_l = _tri(L, True)
    tri_u = _tri(L, False)
    cum_col = _cumsum_rows(tri_l, _log_sigmoid(gcol))
    cum_row = _cumsum_lanes(_log_sigmoid(grow), tri_u)

    fcol = cum_col + carry_col[0:1, :]
    frow = cum_row + carry_row[:, 0:1]
    fcol_ref[0] = fcol
    frow_ref[0] = frow[M_HEADS * 2:, :]
    carry_col[...] = jnp.broadcast_to(fcol[L - 1:L, :], carry_col.shape)
    carry_row[...] = jnp.broadcast_to(frow[:, L - 1:L], carry_row.shape)

    causal = lax.broadcasted_iota(jnp.int32, (L, L), 0) >= lax.broadcasted_iota(jnp.int32, (L, L), 1)
    for hd in range(M_HEADS):
        q = q_ref[0, :, hd * M_DQK:(hd + 1) * M_DQK]
        k = k_ref[0, :, hd * M_DQK:(hd + 1) * M_DQK]
        v = v_ref[0, :, hd * M_DV:(hd + 1) * M_DV]
        ig_c = gcol[:, hd:hd + 1]
        ig_r = grow[hd:hd + 1, :]
        cum_c = cum_col[:, M_HEADS + hd:M_HEADS + hd + 1]
        cum_r = cum_row[M_HEADS + hd:M_HEADS + hd + 1, :]
        m_prev = m_sc[hd, 0:1, 0:1]
        c_st = c_sc[hd]
        n_st = n_sc[hd, 0:1, :]

        dmat = jnp.where(causal, cum_c - cum_r + ig_r, -jnp.inf)
        inter = cum_c + m_prev
        m_t = jnp.maximum(inter, jnp.max(dmat, axis=-1, keepdims=True))
        s = _dot_nt(q, k) * jnp.exp(dmat - m_t)
        g = jnp.exp(inter - m_t)
        qf = q.astype(F32)
        num = _dot(s.astype(BF16), v) + g * _dot(q, c_st.astype(BF16))
        den = jnp.sum(s, axis=-1, keepdims=True) + g * jnp.sum(qf * n_st, axis=-1, keepdims=True)
        h_out = num / jnp.maximum(jnp.abs(den), jnp.exp(-m_t))

        last = cum_c[L - 1:L, :]
        dec = last - cum_c + ig_c
        m_new = jnp.maximum(last + m_prev, jnp.max(dec, axis=0, keepdims=True))
        ws = jnp.exp(dec - m_new)
        gs = jnp.exp(last + m_prev - m_new)
        kw = k.astype(F32) * ws
        c_sc[hd] = gs * c_st + _dot_tn(kw.astype(BF16), v)
        n_sc[hd] = jnp.broadcast_to(gs * n_st + jnp.sum(kw, axis=0, keepdims=True), n_sc.shape[1:])
        m_sc[hd] = jnp.broadcast_to(m_new, m_sc.shape[1:])

        og = og_ref[0, :, hd * M_DV:(hd + 1) * M_DV].astype(F32)
        hn = _rms(h_out) * nw_ref[:, hd * M_DV:(hd + 1) * M_DV]
        y_ref[0, :, hd * M_DV:(hd + 1) * M_DV] = (hn * jax.nn.sigmoid(og)).astype(BF16)


def _mlstm(u, gcol, grow, bias_col, bias_row, norm_w):
    bsz, seq, _ = u.shape
    L = M_CHUNK
    nc = seq // L
    qk_w = M_HEADS * M_DQK
    v_w = M_HEADS * M_DV
    return pl.pallas_call(
        _mlstm_kernel,
        grid=(bsz, nc),
        in_specs=[pl.BlockSpec((1, L, qk_w), lambda b, c: (b, c, 0)),
                  pl.BlockSpec((1, L, qk_w), lambda b, c: (b, c, 1)),
                  pl.BlockSpec((1, L, v_w), lambda b, c: (b, c, 1)),
                  pl.BlockSpec((1, L, v_w), lambda b, c: (b, c, 2)),
                  pl.BlockSpec((1, L, GATE_W), lambda b, c: (b, c, 0)),
                  pl.BlockSpec((N_GATE_ROWS, L), lambda b, c: (0, b * nc + c)),
                  pl.BlockSpec((1, GATE_W), lambda b, c: (0, 0)),
                  pl.BlockSpec((N_GATE_ROWS, LANE), lambda b, c: (0, 0)),
                  pl.BlockSpec((1, v_w), lambda b, c: (0, 0))],
        out_specs=[pl.BlockSpec((1, L, v_w), lambda b, c: (b, c, 0)),
                   pl.BlockSpec((1, L, GATE_W), lambda b, c: (b, c, 0)),
                   pl.BlockSpec((1, F_HEADS, L), lambda b, c: (b, 0, c))],
        out_shape=[jax.ShapeDtypeStruct((bsz, seq, v_w), BF16),
                   jax.ShapeDtypeStruct((bsz, seq, GATE_W), F32),
                   jax.ShapeDtypeStruct((bsz, F_HEADS, seq), F32)],
        scratch_shapes=[pltpu.VMEM((M_HEADS, M_DQK, M_DV), F32),
                        pltpu.VMEM((M_HEADS, SUBLANE, M_DQK), F32),
                        pltpu.VMEM((M_HEADS, SUBLANE, LANE), F32),
                        pltpu.VMEM((SUBLANE, GATE_W), F32),
                        pltpu.VMEM((N_GATE_ROWS, LANE), F32)],
        compiler_params=_cparams(("arbitrary", "arbitrary"), 40),
        name="mlstm",
    )(u, u, u, u, gcol, grow, bias_col, bias_row, norm_w)


def _fox_kernel(q_ref, k_ref, v_ref, fcol_ref, frow_ref, o_ref, k_sc, vt_sc, ck_sc, m_sc, acc_sc, za_sc, zb_sc):
    tq, C = FOX_T, FOX_COLS
    hd = pl.program_id(1)
    qi = pl.program_id(2)
    q = q_ref[0]
    here = pl.ds(pl.multiple_of(qi * tq, tq), tq)

    k_sc[here, :] = k_ref[0]
    vt_sc[0:F_DH, here] = v_ref[0].astype(F32).T.astype(BF16)
    vt_sc[F_DH:, here] = jnp.ones((FOX_ONES, tq), BF16)
    lane = lax.broadcasted_iota(jnp.int32, (tq, GATE_W), 1)
    ck = jnp.sum(jnp.where(lane == 2 * M_HEADS + hd, fcol_ref[0], 0.0), axis=-1, keepdims=True)
    ck_sc[here, :] = jnp.broadcast_to(ck * LOG2E, (tq, LANE))
    cq = frow_ref[0, pl.ds(hd, 1), here] * LOG2E

    m_sc[...] = jnp.full_like(m_sc, -jnp.inf)
    acc_sc[...] = jnp.zeros_like(acc_sc)

    groups = tq // C

    def scores(j, g):
        off = pl.multiple_of(j * tq, tq)
        ckr = ck_sc[pl.ds(off, tq), :]
        return (_dot_nt(k_sc[pl.ds(off, tq), :], q[g * C:(g + 1) * C])
                - jnp.concatenate([ckr] * (C // LANE), axis=1))

    def update(j, g, z):
        off = pl.multiple_of(j * tq, tq)
        cs = slice(g * C, (g + 1) * C)
        nk = z.shape[0]
        m_old = m_sc[:, cs]
        m_new = jnp.maximum(m_old, jnp.max(z, axis=0, keepdims=True) + cq[:, cs])
        p = jnp.exp2(z - (m_new - cq[:, cs]))
        alpha = jnp.exp2(m_old - m_new)
        acc_sc[:, cs] = alpha * acc_sc[:, cs] + _dot(vt_sc[:, pl.ds(off, nk)], p.astype(BF16))
        m_sc[:, cs] = m_new

    def step(j, z_cur, z_nxt):
        for g in range(groups):
            cs = slice(g * C, (g + 1) * C)
            z_nxt[:, cs] = scores(j + 1, g)
            update(j, g, z_cur[:, cs])

    def diagonal(z_ref):
        for g in range(groups):
            nk = (g + 1) * C
            r = lax.broadcasted_iota(jnp.int32, (nk, C), 0)
            c = lax.broadcasted_iota(jnp.int32, (nk, C), 1) + g * C
            update(qi, g, jnp.where(r <= c, z_ref[0:nk, g * C:(g + 1) * C], -jnp.inf))

    for g in range(groups):
        za_sc[:, g * C:(g + 1) * C] = scores(0, g)

    def pair(i, carry):
        step(2 * i, za_sc, zb_sc)
        step(2 * i + 1, zb_sc, za_sc)
        return carry

    lax.fori_loop(0, qi // 2, pair, 0)
    odd = qi % 2 == 1

    @pl.when(odd)
    def _():
        step(qi - 1, za_sc, zb_sc)
        diagonal(zb_sc)

    @pl.when(jnp.logical_not(odd))
    def _():
        diagonal(za_sc)

    o_ref[0] = (acc_sc[0:F_DH, :] / acc_sc[F_DH:F_DH + 1, :]).T.astype(BF16)


def _fox(u, fcol, frow):
    bsz, seq, _ = u.shape
    tq = FOX_T
    q_blk = (M_HEADS * (2 * M_DQK + 2 * M_DV)) // F_DH
    k_blk = q_blk + F_HEADS
    v_blk = k_blk + F_HEADS
    return pl.pallas_call(
        _fox_kernel,
        grid=(bsz, F_HEADS, seq // tq),
        in_specs=[pl.BlockSpec((1, tq, F_DH), lambda b, h, i: (b, i, q_blk + h)),
                  pl.BlockSpec((1, tq, F_DH), lambda b, h, i: (b, i, k_blk + h)),
                  pl.BlockSpec((1, tq, F_DH), lambda b, h, i: (b, i, v_blk + h)),
                  pl.BlockSpec((1, tq, GATE_W), lambda b, h, i: (b, i, 0)),
                  pl.BlockSpec((1, F_HEADS, seq), lambda b, h, i: (b, 0, 0))],
        out_specs=pl.BlockSpec((1, tq, F_DH), lambda b, h, i: (b, i, h)),
        out_shape=jax.ShapeDtypeStruct((bsz, seq, F_HEADS * F_DH), BF16),
        scratch_shapes=[pltpu.VMEM((seq, F_DH), BF16), pltpu.VMEM((F_DH + FOX_ONES, seq), BF16),
                        pltpu.VMEM((seq, LANE), F32),
                        pltpu.VMEM((1, tq), F32), pltpu.VMEM((F_DH + FOX_ONES, tq), F32),
                        pltpu.VMEM((tq, tq), F32), pltpu.VMEM((tq, tq), F32)],
        compiler_params=_cparams(("arbitrary", "arbitrary", "arbitrary"), 48),
        name="fox",
    )(u, u, u, fcol, frow)


def _gla_kernel(q_ref, k_ref, v_ref, og_ref, ga_ref, aw_ref, ab_ref, nw_ref, o_ref, st_sc):
    L, S = G_CHUNK, G_SUB
    nsub = L // S

    @pl.when(pl.program_id(2) == 0)
    def _():
        st_sc[...] = jnp.zeros_like(st_sc)

    tri_l = _tri(L, True)
    row = lax.broadcasted_iota(jnp.int32, (L, 1), 0)
    lane_l = lax.broadcasted_iota(jnp.int32, (S, L), 1)
    row_s = lax.broadcasted_iota(jnp.int32, (S, L), 0)

    def head_chunk(hh, rows):
        dk, dv = slice(hh * G_DK, (hh + 1) * G_DK), slice(hh * G_DV, (hh + 1) * G_DV)
        q = q_ref[0, rows, dk].astype(F32) * (G_DK ** -0.5)
        k = k_ref[0, rows, dk].astype(F32)
        v = v_ref[0, rows, dv]
        z = _dot(ga_ref[0, rows, :].astype(BF16), aw_ref[:, dk]) + ab_ref[:, dk]
        la = _log_sigmoid(z) * (LOG2E / G_TAU)
        c = _cumsum_rows(tri_l, la)
        c_last = c[L - 1:L, :]
        st = st_sc[hh]

        o = _dot_nt((q * jnp.exp2(c)).astype(BF16), st.astype(BF16))

        ref_c = c
        for i in range(1, nsub):
            ref_c = jnp.where(row >= i * S, c[i * S - 1:i * S, :], ref_c)
        qt = (q * jnp.exp2(c - ref_c)).astype(BF16)
        a_rows = [jnp.zeros((S, L), F32)]
        for i in range(1, nsub):
            e = jnp.where(row < i * S, c[i * S - 1:i * S, :] - c, -jnp.inf)
            kt = (k * jnp.exp2(e)).astype(BF16)
            a_rows.append(_dot_nt(qt[i * S:(i + 1) * S, :], kt))

        for i in range(nsub):
            qi = q[i * S:(i + 1) * S, :]
            ci = c[i * S:(i + 1) * S, :]
            a_d = a_rows[i]
            for sl in range(S):
                s_abs = i * S + sl
                e = jnp.minimum(ci - c[s_abs:s_abs + 1, :], 0.0)
                col = jnp.sum(qi * (k[s_abs:s_abs + 1, :] * jnp.exp2(e)), axis=-1, keepdims=True)
                a_d = jnp.where((lane_l == s_abs) & (row_s >= sl), col, a_d)
            a_rows[i] = a_d
        a = jnp.concatenate(a_rows, axis=0)
        o = o + _dot(a.astype(BF16), v)

        kd = (k * jnp.exp2(c_last - c)).astype(BF16)
        st_sc[hh] = st * jnp.exp2(c_last) + _dot_tn(v, kd)

        og = og_ref[0, rows, dv].astype(F32)
        on = _rms(o) * nw_ref[:, dv]
        o_ref[0, rows, dv] = (on * (og * jax.nn.sigmoid(og))).astype(BF16)

    def chunk(cc, carry):
        rows = pl.ds(pl.multiple_of(cc * L, L), L)
        for hh in range(G_HPB):
            head_chunk(hh, rows)
        return carry

    lax.fori_loop(0, G_BLOCK // L, chunk, 0)


def _gla(u, gcol, aw2p, ab, norm_w):
    bsz, seq, _ = u.shape
    blk = G_BLOCK
    wk, wv = G_HPB * G_DK, G_HPB * G_DV
    kq = (G_HEADS * G_DK) // wk
    kv = (2 * G_HEADS * G_DK) // wv
    ko = kv + G_HEADS // G_HPB
    return pl.pallas_call(
        _gla_kernel,
        grid=(bsz, G_HEADS // G_HPB, seq // blk),
        in_specs=[pl.BlockSpec((1, blk, wk), lambda b, h, c: (b, c, h)),
                  pl.BlockSpec((1, blk, wk), lambda b, h, c: (b, c, kq + h)),
                  pl.BlockSpec((1, blk, wv), lambda b, h, c: (b, c, kv + h)),
                  pl.BlockSpec((1, blk, wv), lambda b, h, c: (b, c, ko + h)),
                  pl.BlockSpec((1, blk, GATE_W), lambda b, h, c: (b, c, 0)),
                  pl.BlockSpec((GATE_W, wk), lambda b, h, c: (0, h)),
                  pl.BlockSpec((1, wk), lambda b, h, c: (0, h)),
                  pl.BlockSpec((1, wv), lambda b, h, c: (0, h))],
        out_specs=pl.BlockSpec((1, blk, wv), lambda b, h, c: (b, c, h)),
        out_shape=jax.ShapeDtypeStruct((bsz, seq, G_HEADS * G_DV), BF16),
        scratch_shapes=[pltpu.VMEM((G_HPB, G_DV, G_DK), F32)],
        compiler_params=_cparams(("arbitrary", "arbitrary", "arbitrary"), 40),
        name="gla",
    )(u, u, u, u, gcol, aw2p, ab, norm_w)


def _outproj_kernel(part_ks, has_next, *refs):
    nparts = len(part_ks)
    a_refs = refs[:nparts]
    w_ref, x_ref, pw_ref, gate_ref = refs[nparts:nparts + 4]
    pos = nparts + 4
    if has_next:
        nw_ref, sc_ref, sh_ref = refs[pos:pos + 3]
        pos += 3
    xo_ref = refs[pos]
    pos += 1
    if has_next:
        ho_ref = refs[pos]
        pos += 1

    y, off = None, 0
    for p in range(nparts):
        d = _dot(a_refs[p][...], w_ref[off:off + part_ks[p], :])
        y = d if y is None else y + d
        off += part_ks[p]
    xn = x_ref[...] + _rms(y) * (gate_ref[0] * pw_ref[...])
    xo_ref[...] = xn
    if has_next:
        ho_ref[...] = (_rms(xn) * (nw_ref[...] * (1.0 + sc_ref[0])) + sh_ref[0]).astype(BF16)


def _outproj(parts, w, x2, post_w, gate, nxt, seq, tm=512):
    n, d = x2.shape
    per_b = seq // tm
    part_ks = tuple(p.shape[1] for p in parts)
    ktot = sum(part_ks)
    in_specs = [pl.BlockSpec((tm, kp), lambda i, k: (i, 0)) for kp in part_ks]
    in_specs.append(pl.BlockSpec((ktot, d), lambda i, k: (0, 0), pipeline_mode=pl.Buffered(1)))
    has_next = nxt is not None
    vec = pl.BlockSpec((1, 1, d), lambda i, k: (i // per_b, 0, 0))
    row = pl.BlockSpec((1, d), lambda i, k: (0, 0))
    in_specs += [pl.BlockSpec((tm, d), lambda i, k: (i, 0)), row, vec]
    args = list(parts) + [w, x2, post_w.reshape(1, d), gate]
    out_shape = [jax.ShapeDtypeStruct((n, d), F32)]
    out_specs = [pl.BlockSpec((tm, d), lambda i, k: (i, 0))]
    if has_next:
        nw, sc, sh = nxt
        in_specs += [row, vec, vec]
        args += [nw.reshape(1, d), sc, sh]
        out_shape.append(jax.ShapeDtypeStruct((n, d), BF16))
        out_specs.append(pl.BlockSpec((tm, d), lambda i, k: (i, 0)))
    res = pl.pallas_call(
        functools.partial(_outproj_kernel, part_ks, has_next),
        grid=(n // tm, 1),
        in_specs=in_specs,
        out_specs=out_specs,
        out_shape=out_shape,
        compiler_params=_cparams(("arbitrary", "arbitrary"), 56),
        name="outproj",
    )(*args)
    return (res[0], res[1]) if has_next else (res[0], None)


def _ffn_in_kernel(seq, h_ref, wa32_ref, wg32_ref, cwa_ref, cwg_ref, cba_ref, cbg_ref, o_ref, ea, eg,
                   wa_ref, wg_ref):
    tm = h_ref.shape[0]
    H = CONV_HALO
    m = pl.program_id(1)

    @pl.when(m == 0)
    def _():
        wa_ref[...] = wa32_ref[...].astype(BF16)
        wg_ref[...] = wg32_ref[...].astype(BF16)

    @pl.when((m * tm) % seq == 0)
    def _():
        ea[0:H, :] = jnp.zeros((H, ea.shape[1]), F32)
        eg[0:H, :] = jnp.zeros((H, eg.shape[1]), F32)

    h = h_ref[...]

    def conv(w_ref, cw_ref, cb_ref, ext, scale):
        u = _dot(h, w_ref[...])
        ext[H:H + tm, :] = u
        xe = ext[...]
        u1 = pltpu.roll(xe, 1, 0)[H:, :]
        u2 = pltpu.roll(xe, 2, 0)[H:, :]
        y = ((scale * cw_ref[0:1, :]) * u2 + (scale * cw_ref[1:2, :]) * u1
             + (scale * cw_ref[2:3, :]) * u + scale * cb_ref[...])
        ext[0:H, :] = ext[tm:tm + H, :]
        return y

    a_half = conv(wa_ref, cwa_ref, cba_ref, ea, 0.5)
    g = conv(wg_ref, cwg_ref, cbg_ref, eg, 1.0)
    inner = g * (GELU_C0 + (GELU_C0 * GELU_C1) * (g * g))
    o_ref[...] = ((a_half * g) * (1.0 + jnp.tanh(inner))).astype(BF16)


def _ffn_in(h, w, conv_w, conv_b, seq, tm=1024, tn=512):
    n, d = h.shape
    nj = D_FF // tn
    return pl.pallas_call(
        functools.partial(_ffn_in_kernel, seq),
        grid=(nj, n // tm),
        in_specs=[pl.BlockSpec((tm, d), lambda j, m: (m, 0)),
                  pl.BlockSpec((d, tn), lambda j, m: (0, j)),
                  pl.BlockSpec((d, tn), lambda j, m: (0, nj + j)),
                  pl.BlockSpec((CONV_W, tn), lambda j, m: (0, j)),
                  pl.BlockSpec((CONV_W, tn), lambda j, m: (0, nj + j)),
                  pl.BlockSpec((1, tn), lambda j, m: (0, j)),
                  pl.BlockSpec((1, tn), lambda j, m: (0, nj + j))],
        out_specs=pl.BlockSpec((tm, tn), lambda j, m: (m, j)),
        out_shape=jax.ShapeDtypeStruct((n, D_FF), BF16),
        scratch_shapes=[pltpu.VMEM((tm + CONV_HALO, tn), F32), pltpu.VMEM((tm + CONV_HALO, tn), F32),
                        pltpu.VMEM((d, tn), BF16), pltpu.VMEM((d, tn), BF16)],
        compiler_params=_cparams(("arbitrary", "arbitrary"), 48),
        name="ffn_in",
    )(h, w, w, conv_w, conv_w, conv_b, conv_b)


def _even_weights(w_in, m_gate_b, fox_f_b):
    d = w_in.shape[0]
    qk, vv, fw = M_HEADS * M_DQK, M_HEADS * M_DV, F_HEADS * F_DH
    o = 0
    mq = w_in[:, o:o + qk]; o += qk
    mk = w_in[:, o:o + qk]; o += qk
    mv = w_in[:, o:o + vv]; o += vv
    mo = w_in[:, o:o + vv]; o += vv
    mg = w_in[:, o:o + 2 * M_HEADS]; o += 2 * M_HEADS
    fq = w_in[:, o:o + fw]; o += fw
    fk = w_in[:, o:o + fw]; o += fw
    fv = w_in[:, o:o + fw]; o += fw
    ff = w_in[:, o:o + F_HEADS]
    main = jnp.concatenate([mq * (M_DQK ** -0.5), mk, mv, mo, fq * (F_DH ** -0.5 * LOG2E), fk, fv], axis=1).astype(BF16)
    gates = jnp.concatenate([mg, ff], axis=1)
    wg = jnp.pad(gates, ((0, 0), (0, GATE_W - N_GATE_ROWS))).astype(BF16)
    wgt = gates.T.astype(BF16)
    bias = jnp.concatenate([m_gate_b, fox_f_b])
    bias_col = jnp.pad(bias, (0, GATE_W - N_GATE_ROWS)).reshape(1, GATE_W)
    bias_row = jnp.broadcast_to(bias[:, None], (N_GATE_ROWS, LANE))
    return main, wg, wgt, bias_col, bias_row


def _odd_weights(w_in, a_w2):
    main = w_in[:, :MIX_W].astype(BF16)
    ga = w_in[:, MIX_W:MIX_W + G_RANK]
    wg = jnp.pad(ga, ((0, 0), (0, GATE_W - G_RANK))).astype(BF16)
    wgt = jnp.zeros((N_GATE_ROWS, w_in.shape[0]), BF16)
    aw2p = jnp.pad(a_w2, ((0, GATE_W - G_RANK), (0, 0))).astype(BF16)
    return main, wg, wgt, aw2p


def kernel(x, c, ada_w, ada_b, norm_w, even_in_w, m_gate_b, m_norm_w, fox_f_b, even_out_w, odd_in_w,
           gla_a_w2, gla_a_b, gla_norm_w, odd_out_w, ffn_in_w, ffn_conv_w, ffn_conv_b, ffn_out_w):
    bsz, seq, d = x.shape
    n = bsz * seq
    mod = _ada_mod(c, ada_w, ada_b)
    mod = mod.reshape(DEPTH, bsz, 6, 1, d)

    def mvec(layer, idx):
        return mod[layer, :, idx]

    x2 = x.reshape(n, d)
    h = _prenorm(x2, norm_w[0, 0], mvec(0, 1), mvec(0, 0), seq)
    for layer in range(DEPTH):
        j = layer // 2
        if layer % 2 == 0:
            main, wg, wgt, bias_col, bias_row = _even_weights(even_in_w[j], m_gate_b[j], fox_f_b[j])
            u, gcol, grow = _inproj(h, main, wg, wgt, True)
            u3 = u.reshape(bsz, seq, MIX_W)
            ym, fcol, frow = _mlstm(u3, gcol.reshape(bsz, seq, GATE_W), grow, bias_col, bias_row,
                                    m_norm_w[j].reshape(1, -1))
            yf = _fox(u3, fcol, frow)
            parts = [ym.reshape(n, -1), yf.reshape(n, -1)]
            w_out = even_out_w[j].astype(BF16)
        else:
            main, wg, wgt, aw2p = _odd_weights(odd_in_w[j], gla_a_w2[j])
            u, gcol = _inproj(h, main, wg, wgt, False)
            yo = _gla(u.reshape(bsz, seq, MIX_W), gcol.reshape(bsz, seq, GATE_W), aw2p,
                      gla_a_b[j].reshape(1, -1), gla_norm_w[j].reshape(1, -1))
            parts = [yo.reshape(n, -1)]
            w_out = odd_out_w[j].astype(BF16)
        x2, h = _outproj(parts, w_out, x2, norm_w[layer, 1], mvec(layer, 2),
                         (norm_w[layer, 2], mvec(layer, 4), mvec(layer, 3)), seq)
        a = _ffn_in(h, ffn_in_w[layer], ffn_conv_w[layer].reshape(CONV_W, 2 * D_FF),
                    ffn_conv_b[layer].reshape(1, 2 * D_FF), seq)
        nxt = None
        if layer + 1 < DEPTH:
            nxt = (norm_w[layer + 1, 0], mvec(layer + 1, 1), mvec(layer + 1, 0))
        x2, h = _outproj([a], ffn_out_w[layer].astype(BF16), x2, norm_w[layer, 3], mvec(layer, 5), nxt, seq,
                         tm=FFN_OUT_TM)
    return x2.reshape(bsz, seq, d)
```

```python
import functools

import jax
import jax.numpy as jnp
from jax import lax
from jax.experimental import pallas as pl
from jax.experimental.pallas import tpu as pltpu

F32 = jnp.float32
BF16 = jnp.bfloat16

LANE = 128
SUBLANE = 8
VMEM_BYTES_V7X = 64 * 1024 * 1024

D_MODEL = 2048
DEPTH = 4
EPS = 1e-6
M_HEADS, M_DQK, M_DV = 4, 128, 256
F_HEADS, F_DH = 8, 128
G_HEADS, G_DK, G_DV = 4, 256, 512
G_RANK = 16
G_TAU = 16.0
D_FF = 5632
CONV_W = 3

MIX_W = 6144
GATE_W = LANE
N_GATE_ROWS = 16
M_CHUNK = 256
G_BLOCK = 256
G_CHUNK = 64
G_SUB = 16
G_HPB = 4
FOX_T = 1024
FOX_COLS = 512
FOX_ONES = 16
LOG2E = 1.4426950408889634
CONV_HALO = SUBLANE
GELU_C0 = 0.7978845608028654
GELU_C1 = 0.044715
FFN_OUT_TM = 256
INPROJ_TM = 256


def _cparams(sem, vmem_mb):
    return pltpu.CompilerParams(dimension_semantics=sem, vmem_limit_bytes=vmem_mb * 1024 * 1024)


def _dot(a, b):
    return jnp.dot(a, b, preferred_element_type=F32)


def _dot_nt(a, b):
    return lax.dot_general(a, b, (((1,), (1,)), ((), ())), preferred_element_type=F32)


def _dot_tn(a, b):
    return lax.dot_general(a, b, (((0,), (0,)), ((), ())), preferred_element_type=F32)


def _split3(x):
    hi = x.astype(BF16)
    r = x - hi.astype(F32)
    mid = r.astype(BF16)
    lo = (r - mid.astype(F32)).astype(BF16)
    return hi, mid, lo


def _cumsum_rows(tri_lower, x):
    hi, mid, lo = _split3(x)
    return _dot(tri_lower, hi) + _dot(tri_lower, mid) + _dot(tri_lower, lo)


def _cumsum_lanes(x, tri_upper):
    hi, mid, lo = _split3(x)
    return _dot(hi, tri_upper) + _dot(mid, tri_upper) + _dot(lo, tri_upper)


def _tri(n, lower):
    r = lax.broadcasted_iota(jnp.int32, (n, n), 0)
    c = lax.broadcasted_iota(jnp.int32, (n, n), 1)
    return jnp.where((c <= r) if lower else (r <= c), 1.0, 0.0).astype(BF16)


def _log_sigmoid(z):
    return jnp.minimum(z, 0.0) - jnp.log1p(jnp.exp(-jnp.abs(z)))


def _rms(x):
    return x * lax.rsqrt(jnp.mean(x * x, axis=-1, keepdims=True) + EPS)


def _ada_kernel(ct_ref, w_ref, b_ref, o_ref):
    d, bsz = ct_ref.shape
    tn = w_ref.shape[2]
    rows_per_step = 512

    def body(r, accs):
        rows = pl.ds(pl.multiple_of(r * rows_per_step, rows_per_step), rows_per_step)
        ct = ct_ref[rows, :]
        cond = ct * jax.nn.sigmoid(ct)
        w = w_ref[0, rows, :]
        return tuple(accs[b] + jnp.sum(cond[:, b:b + 1] * w, axis=0, keepdims=True) for b in range(bsz))

    accs = lax.fori_loop(0, d // rows_per_step, body, tuple(jnp.zeros((1, tn), F32) for _ in range(bsz)))
    for b in range(bsz):
        o_ref[0, b:b + 1, :] = accs[b] + b_ref[0]


def _ada_mod(c, ada_w, ada_b):
    bsz, d = c.shape
    depth, _, n = ada_w.shape
    tn = 1024
    return pl.pallas_call(
        _ada_kernel,
        grid=(depth, n // tn),
        in_specs=[pl.BlockSpec((d, bsz), lambda l, j: (0, 0)),
                  pl.BlockSpec((1, d, tn), lambda l, j: (l, 0, j)),
                  pl.BlockSpec((1, 1, tn), lambda l, j: (l, 0, j))],
        out_specs=pl.BlockSpec((1, bsz, tn), lambda l, j: (l, 0, j)),
        out_shape=jax.ShapeDtypeStruct((depth, bsz, n), F32),
        compiler_params=_cparams(("arbitrary", "arbitrary"), 40),
        name="ada_mod",
    )(c.T, ada_w, ada_b.reshape(depth, 1, n))


def _prenorm_kernel(x_ref, w_ref, sc_ref, sh_ref, h_ref):
    y = _rms(x_ref[...]) * w_ref[...]
    h_ref[...] = (y * (1.0 + sc_ref[0]) + sh_ref[0]).astype(BF16)


def _prenorm(x2, w, sc, sh, seq, tm=512):
    n, d = x2.shape
    per_b = seq // tm
    vec = pl.BlockSpec((1, 1, d), lambda i: (i // per_b, 0, 0))
    return pl.pallas_call(
        _prenorm_kernel,
        grid=(n // tm,),
        in_specs=[pl.BlockSpec((tm, d), lambda i: (i, 0)),
                  pl.BlockSpec((1, d), lambda i: (0, 0)), vec, vec],
        out_specs=pl.BlockSpec((tm, d), lambda i: (i, 0)),
        out_shape=jax.ShapeDtypeStruct((n, d), BF16),
        compiler_params=_cparams(("arbitrary",), 32),
        name="prenorm",
    )(x2, w.reshape(1, d), sc, sh)


def _inproj_kernel(h_ref, w_ref, wg_ref, wgt_ref, o_ref, gcol_ref, *grow_ref):
    h = h_ref[...]
    o_ref[...] = _dot(h, w_ref[...]).astype(BF16)

    @pl.when(pl.program_id(1) == 0)
    def _():
        gcol_ref[...] = _dot(h, wg_ref[...])
        if grow_ref:
            grow_ref[0][...] = _dot_nt(wgt_ref[...], h)


def _inproj(h, w, wg, wgt, want_rows, tm=INPROJ_TM, tn=MIX_W):
    n, d = h.shape
    ncol = w.shape[1]
    out_shape = [jax.ShapeDtypeStruct((n, ncol), BF16), jax.ShapeDtypeStruct((n, GATE_W), F32)]
    out_specs = [pl.BlockSpec((tm, tn), lambda i, j: (i, j)),
                 pl.BlockSpec((tm, GATE_W), lambda i, j: (i, 0))]
    if want_rows:
        out_shape.append(jax.ShapeDtypeStruct((N_GATE_ROWS, n), F32))
        out_specs.append(pl.BlockSpec((N_GATE_ROWS, tm), lambda i, j: (0, i)))
    return pl.pallas_call(
        _inproj_kernel,
        grid=(n // tm, ncol // tn),
        in_specs=[pl.BlockSpec((tm, d), lambda i, j: (i, 0)),
                  pl.BlockSpec((d, tn), lambda i, j: (0, j), pipeline_mode=pl.Buffered(1)),
                  pl.BlockSpec((d, GATE_W), lambda i, j: (0, 0), pipeline_mode=pl.Buffered(1)),
                  pl.BlockSpec((N_GATE_ROWS, d), lambda i, j: (0, 0), pipeline_mode=pl.Buffered(1))],
        out_specs=out_specs,
        out_shape=out_shape,
        compiler_params=_cparams(("arbitrary", "arbitrary"), 40),
        name="inproj",
    )(h, w, wg, wgt)


def _mlstm_kernel(q_ref, k_ref, v_ref, og_ref, gcol_ref, grow_ref, bcol_ref, brow_ref, nw_ref,
                  y_ref, fcol_ref, frow_ref, c_sc, n_sc, m_sc, carry_col, carry_row):
    L = M_CHUNK

    @pl.when(pl.program_id(1) == 0)
    def _():
        c_sc[...] = jnp.zeros_like(c_sc)
        n_sc[...] = jnp.zeros_like(n_sc)
        m_sc[...] = jnp.zeros_like(m_sc)
        carry_col[...] = jnp.zeros_like(carry_col)
        carry_row[...] = jnp.zeros_like(carry_row)

    gcol = gcol_ref[0] + bcol_ref[...]
    grow = grow_ref[...] + brow_ref[:, 0:1]
    tri_l = _tri(L, True)
    tri_u = _tri(L, False)
    cum_col = _cumsum_rows(tri_l, _log_sigmoid(gcol))
    cum_row = _cumsum_lanes(_log_sigmoid(grow), tri_u)

    fcol = cum_col + carry_col[0:1, :]
    frow = cum_row + carry_row[:, 0:1]
    fcol_ref[0] = fcol
    frow_ref[0] = frow[M_HEADS * 2:, :]
    carry_col[...] = jnp.broadcast_to(fcol[L - 1:L, :], carry_col.shape)
    carry_row[...] = jnp.broadcast_to(frow[:, L - 1:L], carry_row.shape)

    causal = lax.broadcasted_iota(jnp.int32, (L, L), 0) >= lax.broadcasted_iota(jnp.int32, (L, L), 1)
    for hd in range(M_HEADS):
        q = q_ref[0, :, hd * M_DQK:(hd + 1) * M_DQK]
        k = k_ref[0, :, hd * M_DQK:(hd + 1) * M_DQK]
        v = v_ref[0, :, hd * M_DV:(hd + 1) * M_DV]
        ig_c = gcol[:, hd:hd + 1]
        ig_r = grow[hd:hd + 1, :]
        cum_c = cum_col[:, M_HEADS + hd:M_HEADS + hd + 1]
        cum_r = cum_row[M_HEADS + hd:M_HEADS + hd + 1, :]
        m_prev = m_sc[hd, 0:1, 0:1]
        c_st = c_sc[hd]
        n_st = n_sc[hd, 0:1, :]

        dmat = jnp.where(causal, cum_c - cum_r + ig_r, -jnp.inf)
        inter = cum_c + m_prev
        m_t = jnp.maximum(inter, jnp.max(dmat, axis=-1, keepdims=True))
        s = _dot_nt(q, k) * jnp.exp(dmat - m_t)
        g = jnp.exp(inter - m_t)
        qf = q.astype(F32)
        num = _dot(s.astype(BF16), v) + g * _dot(q, c_st.astype(BF16))
        den = jnp.sum(s, axis=-1, keepdims=True) + g * jnp.sum(qf * n_st, axis=-1, keepdims=True)
        h_out = num / jnp.maximum(jnp.abs(den), jnp.exp(-m_t))

        last = cum_c[L - 1:L, :]
        dec = last - cum_c + ig_c
        m_new = jnp.maximum(last + m_prev, jnp.max(dec, axis=0, keepdims=True))
        ws = jnp.exp(dec - m_new)
        gs = jnp.exp(last + m_prev - m_new)
        kw = k.astype(F32) * ws
        c_sc[hd] = gs * c_st + _dot_tn(kw.astype(BF16), v)
        n_sc[hd] = jnp.broadcast_to(gs * n_st + jnp.sum(kw, axis=0, keepdims=True), n_sc.shape[1:])
        m_sc[hd] = jnp.broadcast_to(m_new, m_sc.shape[1:])

        og = og_ref[0, :, hd * M_DV:(hd + 1) * M_DV].astype(F32)
        hn = _rms(h_out) * nw_ref[:, hd * M_DV:(hd + 1) * M_DV]
        y_ref[0, :, hd * M_DV:(hd + 1) * M_DV] = (hn * jax.nn.sigmoid(og)).astype(BF16)


def _mlstm(u, gcol, grow, bias_col, bias_row, norm_w):
    bsz, seq, _ = u.shape
    L = M_CHUNK
    nc = seq // L
    qk_w = M_HEADS * M_DQK
    v_w = M_HEADS * M_DV
    return pl.pallas_call(
        _mlstm_kernel,
        grid=(bsz, nc),
        in_specs=[pl.BlockSpec((1, L, qk_w), lambda b, c: (b, c, 0)),
                  pl.BlockSpec((1, L, qk_w), lambda b, c: (b, c, 1)),
                  pl.BlockSpec((1, L, v_w), lambda b, c: (b, c, 1)),
                  pl.BlockSpec((1, L, v_w), lambda b, c: (b, c, 2)),
                  pl.BlockSpec((1, L, GATE_W), lambda b, c: (b, c, 0)),
                  pl.BlockSpec((N_GATE_ROWS, L), lambda b, c: (0, b * nc + c)),
                  pl.BlockSpec((1, GATE_W), lambda b, c: (0, 0)),
                  pl.BlockSpec((N_GATE_ROWS, LANE), lambda b, c: (0, 0)),
                  pl.BlockSpec((1, v_w), lambda b, c: (0, 0))],
        out_specs=[pl.BlockSpec((1, L, v_w), lambda b, c: (b, c, 0)),
                   pl.BlockSpec((1, L, GATE_W), lambda b, c: (b, c, 0)),
                   pl.BlockSpec((1, F_HEADS, L), lambda b, c: (b, 0, c))],
        out_shape=[jax.ShapeDtypeStruct((bsz, seq, v_w), BF16),
                   jax.ShapeDtypeStruct((bsz, seq, GATE_W), F32),
                   jax.ShapeDtypeStruct((bsz, F_HEADS, seq), F32)],
        scratch_shapes=[pltpu.VMEM((M_HEADS, M_DQK, M_DV), F32),
                        pltpu.VMEM((M_HEADS, SUBLANE, M_DQK), F32),
                        pltpu.VMEM((M_HEADS, SUBLANE, LANE), F32),
                        pltpu.VMEM((SUBLANE, GATE_W), F32),
                        pltpu.VMEM((N_GATE_ROWS, LANE), F32)],
        compiler_params=_cparams(("arbitrary", "arbitrary"), 40),
        name="mlstm",
    )(u, u, u, u, gcol, grow, bias_col, bias_row, norm_w)


def _fox_kernel(q_ref, k_ref, v_ref, fcol_ref, frow_ref, o_ref, k_sc, vt_sc, ck_sc, m_sc, acc_sc, za_sc, zb_sc):
    tq, C = FOX_T, FOX_COLS
    hd = pl.program_id(1)
    qi = pl.program_id(2)
    q = q_ref[0]
    here = pl.ds(pl.multiple_of(qi * tq, tq), tq)

    k_sc[here, :] = k_ref[0]
    vt_sc[0:F_DH, here] = v_ref[0].astype(F32).T.astype(BF16)
    vt_sc[F_DH:, here] = jnp.ones((FOX_ONES, tq), BF16)
    lane = lax.broadcasted_iota(jnp.int32, (tq, GATE_W), 1)
    ck = jnp.sum(jnp.where(lane == 2 * M_HEADS + hd, fcol_ref[0], 0.0), axis=-1, keepdims=True)
    ck_sc[here, :] = jnp.broadcast_to(ck * LOG2E, (tq, LANE))
    cq = frow_ref[0, pl.ds(hd, 1), here] * LOG2E

    m_sc[...] = jnp.full_like(m_sc, -jnp.inf)
    acc_sc[...] = jnp.zeros_like(acc_sc)

    groups = tq // C

    def scores(j, g):
        off = pl.multiple_of(j * tq, tq)
        ckr = ck_sc[pl.ds(off, tq), :]
        return (_dot_nt(k_sc[pl.ds(off, tq), :], q[g * C:(g + 1) * C])
                - jnp.concatenate([ckr] * (C // LANE), axis=1))

    def update(j, g, z):
        off = pl.multiple_of(j * tq, tq)
        cs = slice(g * C, (g + 1) * C)
        nk = z.shape[0]
        m_old = m_sc[:, cs]
        m_new = jnp.maximum(m_old, jnp.max(z, axis=0, keepdims=True) + cq[:, cs])
        p = jnp.exp2(z - (m_new - cq[:, cs]))
        alpha = jnp.exp2(m_old - m_new)
        acc_sc[:, cs] = alpha * acc_sc[:, cs] + _dot(vt_sc[:, pl.ds(off, nk)], p.astype(BF16))
        m_sc[:, cs] = m_new

    def step(j, z_cur, z_nxt):
        for g in range(groups):
            cs = slice(g * C, (g + 1) * C)
            z_nxt[:, cs] = scores(j + 1, g)
            update(j, g, z_cur[:, cs])

    def diagonal(z_ref):
        for g in range(groups):
            nk = (g + 1) * C
            r = lax.broadcasted_iota(jnp.int32, (nk, C), 0)
            c = lax.broadcasted_iota(jnp.int32, (nk, C), 1) + g * C
            update(qi, g, jnp.where(r <= c, z_ref[0:nk, g * C:(g + 1) * C], -jnp.inf))

    for g in range(groups):
        za_sc[:, g * C:(g + 1) * C] = scores(0, g)

    def pair(i, carry):
        step(2 * i, za_sc, zb_sc)
        step(2 * i + 1, zb_sc, za_sc)
        return carry

    lax.fori_loop(0, qi // 2, pair, 0)
    odd = qi % 2 == 1

    @pl.when(odd)
    def _():
        step(qi - 1, za_sc, zb_sc)
        diagonal(zb_sc)

    @pl.when(jnp.logical_not(odd))
    def _():
        diagonal(za_sc)

    o_ref[0] = (acc_sc[0:F_DH, :] / acc_sc[F_DH:F_DH + 1, :]).T.astype(BF16)


def _fox(u, fcol, frow):
    bsz, seq, _ = u.shape
    tq = FOX_T
    q_blk = (M_HEADS * (2 * M_DQK + 2 * M_DV)) // F_DH
    k_blk = q_blk + F_HEADS
    v_blk = k_blk + F_HEADS
    return pl.pallas_call(
        _fox_kernel,
        grid=(bsz, F_HEADS, seq // tq),
        in_specs=[pl.BlockSpec((1, tq, F_DH), lambda b, h, i: (b, i, q_blk + h)),
                  pl.BlockSpec((1, tq, F_DH), lambda b, h, i: (b, i, k_blk + h)),
                  pl.BlockSpec((1, tq, F_DH), lambda b, h, i: (b, i, v_blk + h)),
                  pl.BlockSpec((1, tq, GATE_W), lambda b, h, i: (b, i, 0)),
                  pl.BlockSpec((1, F_HEADS, seq), lambda b, h, i: (b, 0, 0))],
        out_specs=pl.BlockSpec((1, tq, F_DH), lambda b, h, i: (b, i, h)),
        out_shape=jax.ShapeDtypeStruct((bsz, seq, F_HEADS * F_DH), BF16),
        scratch_shapes=[pltpu.VMEM((seq, F_DH), BF16), pltpu.VMEM((F_DH + FOX_ONES, seq), BF16),
                        pltpu.VMEM((seq, LANE), F32),
                        pltpu.VMEM((1, tq), F32), pltpu.VMEM((F_DH + FOX_ONES, tq), F32),
                        pltpu.VMEM((tq, tq), F32), pltpu.VMEM((tq, tq), F32)],
        compiler_params=_cparams(("arbitrary", "arbitrary", "arbitrary"), 48),
        name="fox",
    )(u, u, u, fcol, frow)


def _gla_kernel(q_ref, k_ref, v_ref, og_ref, ga_ref, aw_ref, ab_ref, nw_ref, o_ref, st_sc):
    L, S = G_CHUNK, G_SUB
    nsub = L // S

    @pl.when(pl.program_id(2) == 0)
    def _():
        st_sc[...] = jnp.zeros_like(st_sc)

    tri_l = _tri(L, True)
    row = lax.broadcasted_iota(jnp.int32, (L, 1), 0)
    lane_l = lax.broadcasted_iota(jnp.int32, (S, L), 1)
    row_s = lax.broadcasted_iota(jnp.int32, (S, L), 0)

    def head_chunk(hh, rows):
        dk, dv = slice(hh * G_DK, (hh + 1) * G_DK), slice(hh * G_DV, (hh + 1) * G_DV)
        q = q_ref[0, rows, dk].astype(F32) * (G_DK ** -0.5)
        k = k_ref[0, rows, dk].astype(F32)
        v = v_ref[0, rows, dv]
        z = _dot(ga_ref[0, rows, :].astype(BF16), aw_ref[:, dk]) + ab_ref[:, dk]
        la = _log_sigmoid(z) * (LOG2E / G_TAU)
        c = _cumsum_rows(tri_l, la)
        c_last = c[L - 1:L, :]
        st = st_sc[hh]

        o = _dot_nt((q * jnp.exp2(c)).astype(BF16), st.astype(BF16))

        ref_c = c
        for i in range(1, nsub):
            ref_c = jnp.where(row >= i * S, c[i * S - 1:i * S, :], ref_c)
        qt = (q * jnp.exp2(c - ref_c)).astype(BF16)
        a_rows = [jnp.zeros((S, L), F32)]
        for i in range(1, nsub):
            e = jnp.where(row < i * S, c[i * S - 1:i * S, :] - c, -jnp.inf)
            kt = (k * jnp.exp2(e)).astype(BF16)
            a_rows.append(_dot_nt(qt[i * S:(i + 1) * S, :], kt))

        for i in range(nsub):
            qi = q[i * S:(i + 1) * S, :]
            ci = c[i * S:(i + 1) * S, :]
            a_d = a_rows[i]
            for sl in range(S):
                s_abs = i * S + sl
                e = jnp.minimum(ci - c[s_abs:s_abs + 1, :], 0.0)
                col = jnp.sum(qi * (k[s_abs:s_abs + 1, :] * jnp.exp2(e)), axis=-1, keepdims=True)
                a_d = jnp.where((lane_l == s_abs) & (row_s >= sl), col, a_d)
            a_rows[i] = a_d
        a = jnp.concatenate(a_rows, axis=0)
        o = o + _dot(a.astype(BF16), v)

        kd = (k * jnp.exp2(c_last - c)).astype(BF16)
        st_sc[hh] = st * jnp.exp2(c_last) + _dot_tn(v, kd)

        og = og_ref[0, rows, dv].astype(F32)
        on = _rms(o) * nw_ref[:, dv]
        o_ref[0, rows, dv] = (on * (og * jax.nn.sigmoid(og))).astype(BF16)

    def chunk(cc, carry):
        rows = pl.ds(pl.multiple_of(cc * L, L), L)
        for hh in range(G_HPB):
            head_chunk(hh, rows)
        return carry

    lax.fori_loop(0, G_BLOCK // L, chunk, 0)


def _gla(u, gcol, aw2p, ab, norm_w):
    bsz, seq, _ = u.shape
    blk = G_BLOCK
    wk, wv = G_HPB * G_DK, G_HPB * G_DV
    kq = (G_HEADS * G_DK) // wk
    kv = (2 * G_HEADS * G_DK) // wv
    ko = kv + G_HEADS // G_HPB
    return pl.pallas_call(
        _gla_kernel,
        grid=(bsz, G_HEADS // G_HPB, seq // blk),
        in_specs=[pl.BlockSpec((1, blk, wk), lambda b, h, c: (b, c, h)),
                  pl.BlockSpec((1, blk, wk), lambda b, h, c: (b, c, kq + h)),
                  pl.BlockSpec((1, blk, wv), lambda b, h, c: (b, c, kv + h)),
                  pl.BlockSpec((1, blk, wv), lambda b, h, c: (b, c, ko + h)),
                  pl.BlockSpec((1, blk, GATE_W), lambda b, h, c: (b, c, 0)),
                  pl.BlockSpec((GATE_W, wk), lambda b, h, c: (0, h)),
                  pl.BlockSpec((1, wk), lambda b, h, c: (0, h)),
                  pl.BlockSpec((1, wv), lambda b, h, c: (0, h))],
        out_specs=pl.BlockSpec((1, blk, wv), lambda b, h, c: (b, c, h)),
        out_shape=jax.ShapeDtypeStruct((bsz, seq, G_HEADS * G_DV), BF16),
        scratch_shapes=[pltpu.VMEM((G_HPB, G_DV, G_DK), F32)],
        compiler_params=_cparams(("arbitrary", "arbitrary", "arbitrary"), 40),
        name="gla",
    )(u, u, u, u, gcol, aw2p, ab, norm_w)


def _outproj_kernel(part_ks, has_next, *refs):
    nparts = len(part_ks)
    a_refs = refs[:nparts]
    w_ref, x_ref, pw_ref, gate_ref = refs[nparts:nparts + 4]
    pos = nparts + 4
    if has_next:
        nw_ref, sc_ref, sh_ref = refs[pos:pos + 3]
        pos += 3
    xo_ref = refs[pos]
    pos += 1
    if has_next:
        ho_ref = refs[pos]
        pos += 1

    y, off = None, 0
    for p in range(nparts):
        d = _dot(a_refs[p][...], w_ref[off:off + part_ks[p], :])
        y = d if y is None else y + d
        off += part_ks[p]
    xn = x_ref[...] + _rms(y) * (gate_ref[0] * pw_ref[...])
    xo_ref[...] = xn
    if has_next:
        ho_ref[...] = (_rms(xn) * (nw_ref[...] * (1.0 + sc_ref[0])) + sh_ref[0]).astype(BF16)


def _outproj(parts, w, x2, post_w, gate, nxt, seq, tm=512):
    n, d = x2.shape
    per_b = seq // tm
    part_ks = tuple(p.shape[1] for p in parts)
    ktot = sum(part_ks)
    in_specs = [pl.BlockSpec((tm, kp), lambda i, k: (i, 0)) for kp in part_ks]
    in_specs.append(pl.BlockSpec((ktot, d), lambda i, k: (0, 0), pipeline_mode=pl.Buffered(1)))
    has_next = nxt is not None
    vec = pl.BlockSpec((1, 1, d), lambda i, k: (i // per_b, 0, 0))
    row = pl.BlockSpec((1, d), lambda i, k: (0, 0))
    in_specs += [pl.BlockSpec((tm, d), lambda i, k: (i, 0)), row, vec]
    args = list(parts) + [w, x2, post_w.reshape(1, d), gate]
    out_shape = [jax.ShapeDtypeStruct((n, d), F32)]
    out_specs = [pl.BlockSpec((tm, d), lambda i, k: (i, 0))]
    if has_next:
        nw, sc, sh = nxt
        in_specs += [row, vec, vec]
        args += [nw.reshape(1, d), sc, sh]
        out_shape.append(jax.ShapeDtypeStruct((n, d), BF16))
        out_specs.append(pl.BlockSpec((tm, d), lambda i, k: (i, 0)))
    res = pl.pallas_call(
        functools.partial(_outproj_kernel, part_ks, has_next),
        grid=(n // tm, 1),
        in_specs=in_specs,
        out_specs=out_specs,
        out_shape=out_shape,
        compiler_params=_cparams(("arbitrary", "arbitrary"), 56),
        name="outproj",
    )(*args)
    return (res[0], res[1]) if has_next else (res[0], None)


def _ffn_in_kernel(seq, h_ref, wa32_ref, wg32_ref, cwa_ref, cwg_ref, cba_ref, cbg_ref, o_ref, ea, eg,
                   wa_ref, wg_ref):
    tm = h_ref.shape[0]
    H = CONV_HALO
    m = pl.program_id(1)

    @pl.when(m == 0)
    def _():
        wa_ref[...] = wa32_ref[...].astype(BF16)
        wg_ref[...] = wg32_ref[...].astype(BF16)

    @pl.when((m * tm) % seq == 0)
    def _():
        ea[0:H, :] = jnp.zeros((H, ea.shape[1]), F32)
        eg[0:H, :] = jnp.zeros((H, eg.shape[1]), F32)

    h = h_ref[...]

    def conv(w_ref, cw_ref, cb_ref, ext, scale):
        u = _dot(h, w_ref[...])
        ext[H:H + tm, :] = u
        xe = ext[...]
        u1 = pltpu.roll(xe, 1, 0)[H:, :]
        u2 = pltpu.roll(xe, 2, 0)[H:, :]
        y = ((scale * cw_ref[0:1, :]) * u2 + (scale * cw_ref[1:2, :]) * u1
             + (scale * cw_ref[2:3, :]) * u + scale * cb_ref[...])
        ext[0:H, :] = ext[tm:tm + H, :]
        return y

    a_half = conv(wa_ref, cwa_ref, cba_ref, ea, 0.5)
    g = conv(wg_ref, cwg_ref, cbg_ref, eg, 1.0)
    inner = g * (GELU_C0 + (GELU_C0 * GELU_C1) * (g * g))
    o_ref[...] = ((a_half * g) * (1.0 + jnp.tanh(inner))).astype(BF16)


def _ffn_in(h, w, conv_w, conv_b, seq, tm=1024, tn=512):
    n, d = h.shape
    nj = D_FF // tn
    return pl.pallas_call(
        functools.partial(_ffn_in_kernel, seq),
        grid=(nj, n // tm),
        in_specs=[pl.BlockSpec((tm, d), lambda j, m: (m, 0)),
                  pl.BlockSpec((d, tn), lambda j, m: (0, j)),
                  pl.BlockSpec((d, tn), lambda j, m: (0, nj + j)),
                  pl.BlockSpec((CONV_W, tn), lambda j, m: (0, j)),
                  pl.BlockSpec((CONV_W, tn), lambda j, m: (0, nj + j)),
                  pl.BlockSpec((1, tn), lambda j, m: (0, j)),
                  pl.BlockSpec((1, tn), lambda j, m: (0, nj + j))],
        out_specs=pl.BlockSpec((tm, tn), lambda j, m: (m, j)),
        out_shape=jax.ShapeDtypeStruct((n, D_FF), BF16),
        scratch_shapes=[pltpu.VMEM((tm + CONV_HALO, tn), F32), pltpu.VMEM((tm + CONV_HALO, tn), F32),
                        pltpu.VMEM((d, tn), BF16), pltpu.VMEM((d, tn), BF16)],
        compiler_params=_cparams(("arbitrary", "arbitrary"), 48),
        name="ffn_in",
    )(h, w, w, conv_w, conv_w, conv_b, conv_b)


def _even_weights(w_in, m_gate_b, fox_f_b):
    d = w_in.shape[0]
    qk, vv, fw = M_HEADS * M_DQK, M_HEADS * M_DV, F_HEADS * F_DH
    o = 0
    mq = w_in[:, o:o + qk]; o += qk
    mk = w_in[:, o:o + qk]; o += qk
    mv = w_in[:, o:o + vv]; o += vv
    mo = w_in[:, o:o + vv]; o += vv
    mg = w_in[:, o:o + 2 * M_HEADS]; o += 2 * M_HEADS
    fq = w_in[:, o:o + fw]; o += fw
    fk = w_in[:, o:o + fw]; o += fw
    fv = w_in[:, o:o + fw]; o += fw
    ff = w_in[:, o:o + F_HEADS]
    main = jnp.concatenate([mq * (M_DQK ** -0.5), mk, mv, mo, fq * (F_DH ** -0.5 * LOG2E), fk, fv], axis=1).astype(BF16)
    gates = jnp.concatenate([mg, ff], axis=1)
    wg = jnp.pad(gates, ((0, 0), (0, GATE_W - N_GATE_ROWS))).astype(BF16)
    wgt = gates.T.astype(BF16)
    bias = jnp.concatenate([m_gate_b, fox_f_b])
    bias_col = jnp.pad(bias, (0, GATE_W - N_GATE_ROWS)).reshape(1, GATE_W)
    bias_row = jnp.broadcast_to(bias[:, None], (N_GATE_ROWS, LANE))
    return main, wg, wgt, bias_col, bias_row


def _odd_weights(w_in, a_w2):
    main = w_in[:, :MIX_W].astype(BF16)
    ga = w_in[:, MIX_W:MIX_W + G_RANK]
    wg = jnp.pad(ga, ((0, 0), (0, GATE_W - G_RANK))).astype(BF16)
    wgt = jnp.zeros((N_GATE_ROWS, w_in.shape[0]), BF16)
    aw2p = jnp.pad(a_w2, ((0, GATE_W - G_RANK), (0, 0))).astype(BF16)
    return main, wg, wgt, aw2p


def kernel(x, c, ada_w, ada_b, norm_w, even_in_w, m_gate_b, m_norm_w, fox_f_b, even_out_w, odd_in_w,
           gla_a_w2, gla_a_b, gla_norm_w, odd_out_w, ffn_in_w, ffn_conv_w, ffn_conv_b, ffn_out_w):
    bsz, seq, d = x.shape
    n = bsz * seq
    mod = _ada_mod(c, ada_w, ada_b)
    mod = mod.reshape(DEPTH, bsz, 6, 1, d)

    def mvec(layer, idx):
        return mod[layer, :, idx]

    x2 = x.reshape(n, d)
    h = _prenorm(x2, norm_w[0, 0], mvec(0, 1), mvec(0, 0), seq)
    for layer in range(DEPTH):
        j = layer // 2
        if layer % 2 == 0:
            main, wg, wgt, bias_col, bias_row = _even_weights(even_in_w[j], m_gate_b[j], fox_f_b[j])
            u, gcol, grow = _inproj(h, main, wg, wgt, True)
            u3 = u.reshape(bsz, seq, MIX_W)
            ym, fcol, frow = _mlstm(u3, gcol.reshape(bsz, seq, GATE_W), grow, bias_col, bias_row,
                                    m_norm_w[j].reshape(1, -1))
            yf = _fox(u3, fcol, frow)
            parts = [ym.reshape(n, -1), yf.reshape(n, -1)]
            w_out = even_out_w[j].astype(BF16)
        else:
            main, wg, wgt, aw2p = _odd_weights(odd_in_w[j], gla_a_w2[j])
            u, gcol = _inproj(h, main, wg, wgt, False)
            yo = _gla(u.reshape(bsz, seq, MIX_W), gcol.reshape(bsz, seq, GATE_W), aw2p,
                      gla_a_b[j].reshape(1, -1), gla_norm_w[j].reshape(1, -1))
            parts = [yo.reshape(n, -1)]
            w_out = odd_out_w[j].astype(BF16)
        x2, h = _outproj(parts, w_out, x2, norm_w[layer, 1], mvec(layer, 2),
                         (norm_w[layer, 2], mvec(layer, 4), mvec(layer, 3)), seq)
        a = _ffn_in(h, ffn_in_w[layer], ffn_conv_w[layer].reshape(CONV_W, 2 * D_FF),
                    ffn_conv_b[layer].reshape(1, 2 * D_FF), seq)
        nxt = None
        if layer + 1 < DEPTH:
            nxt = (norm_w[layer + 1, 0], mvec(layer + 1, 1), mvec(layer + 1, 0))
        x2, h = _outproj([a], ffn_out_w[layer].astype(BF16), x2, norm_w[layer, 3], mvec(layer, 5), nxt, seq,
                         tm=FFN_OUT_TM)
    return x2.reshape(bsz, seq, d)
```
